```python
import jax, jax.numpy as jnp
from jax import lax
import numpy as np

D_MODEL = 2048
BATCH = 32
SEQ = 256
DEPTH = 2
DEC_BATCH = 4
DEC_SEQ = 4096
PAST_LEN = 256

GRID_W = 64
H_A = 8
DK = 128
DV = 128
W_A = H_A * DK
N_FG = 4
FG = 256
W_B = N_FG * FG
CHUNK = 32
D_FF = -(-8 * D_MODEL // (3 * 256)) * 256
N_IN = 5 * W_A + W_B + 2 * D_MODEL
EPS = 1e-6
POS_BASE = 10000.0

kernel_name = "hybrid_hgrn2_fnet_diffusion_step"


def rmsnorm(x, w):
    xf = x.astype(jnp.float32)
    y = xf * lax.rsqrt(jnp.mean(xf * xf, axis=-1, keepdims=True) + EPS)
    return (y * w.astype(jnp.float32)).astype(x.dtype)


def adaln(cvec, w_mod_l, b_mod_l):
    m = (jax.nn.silu(cvec) @ w_mod_l + b_mod_l).reshape(cvec.shape[0], 1, 6, D_MODEL)
    return tuple(m[:, :, k] for k in range(6))


def grid_pos_embed(n, dtype):
    rows = n // GRID_W
    r, col = jnp.meshgrid(jnp.arange(rows, dtype=jnp.float32), jnp.arange(GRID_W, dtype=jnp.float32), indexing="ij")
    quarter = D_MODEL // 4
    omega = 1.0 / (POS_BASE ** (jnp.arange(quarter, dtype=jnp.float32) / quarter))
    def enc(p):
        a = p.reshape(-1)[:, None] * omega[None, :]
        return jnp.concatenate([jnp.sin(a), jnp.cos(a)], axis=-1)
    return jnp.concatenate([enc(r), enc(col)], axis=-1).astype(dtype)


def chunk_scan(q, k, v, logf, s0):
    bsz, n, h, _ = q.shape
    nc = n // CHUNK
    def to_chunks(t):
        return t.reshape(bsz, nc, CHUNK, h, t.shape[-1]).transpose(1, 0, 3, 2, 4)
    tri = jnp.tril(jnp.ones((CHUNK, CHUNK), dtype=bool))
    def step(S, xs):
        qc, kc, vc, gc = xs
        b = jnp.cumsum(gc, axis=-2)
        rel = jnp.where(tri[:, :, None], b[..., :, None, :] - b[..., None, :, :], -jnp.inf)
        a = jnp.einsum('bhtd,bhsd,bhtsd->bhts', qc, kc, jnp.exp(rel))
        o = jnp.einsum('bhts,bhsv->bhtv', a, vc) + jnp.einsum('bhtd,bhdv->bhtv', qc * jnp.exp(b), S)
        b_last = b[..., -1:, :]
        S = jnp.exp(b_last[..., 0, :])[..., None] * S + jnp.einsum('bhsd,bhsv->bhdv', kc * jnp.exp(b_last - b), vc)
        return S, o
    S, o = lax.scan(step, s0, (to_chunks(q), to_chunks(k), to_chunks(v), to_chunks(logf)))
    o = o.transpose(1, 0, 3, 2, 4).reshape(bsz, n, h, v.shape[-1])
    return o, S


def mixer(h, w_in_l, lb_l, g_norm_l, w_a_l, w_b_l, w_out_l, s0f, s0b):
    bsz, n, _ = h.shape
    p = h @ w_in_l
    q, ff, fb, iv, og, u, ga, gb = jnp.split(
        p, [W_A, 2 * W_A, 3 * W_A, 4 * W_A, 5 * W_A, 5 * W_A + W_B, 5 * W_A + W_B + D_MODEL], axis=-1)
    def heads(t):
        return t.astype(jnp.float32).reshape(bsz, n, H_A, -1)
    qh, vh = heads(q), heads(iv)
    def gates(fr, lbd):
        lbh = lbd.reshape(H_A, DK)
        f = lbh + (1.0 - lbh) * jax.nn.sigmoid(heads(fr))
        return 1.0 - f, jnp.log(f)
    kf, gf = gates(ff, lb_l[0])
    kb, gbw = gates(fb, lb_l[1])
    of, sf = chunk_scan(qh, kf, vh, gf, s0f)
    flip = lambda t: jnp.flip(t, axis=1)
    ob, sb = chunk_scan(flip(qh), flip(kb), flip(vh), flip(gbw), s0b)
    o = of + flip(ob)
    o = o * lax.rsqrt(jnp.mean(o * o, axis=-1, keepdims=True) + EPS) * g_norm_l.astype(jnp.float32)
    o = o.reshape(bsz, n, W_A).astype(h.dtype) * jax.nn.silu(og)
    uf = u.astype(jnp.float32).reshape(bsz, n, N_FG, FG)
    z = jnp.fft.fft2(uf, axes=(1, 3), norm="ortho").real.reshape(bsz, n, W_B).astype(h.dtype)
    merged = jax.nn.sigmoid(ga) * (o @ w_a_l) + jax.nn.sigmoid(gb) * (z @ w_b_l)
    return merged @ w_out_l, sf, sb


def layer(x, mod, n1, n2, w_in_l, lb_l, g_norm_l, w_a_l, w_b_l, w_out_l, w_ff_in_l, w_ff_out_l, s0f, s0b):
    sh1, sc1, g1, sh2, sc2, g2 = mod
    h = rmsnorm(x, n1) * (1.0 + sc1) + sh1
    m, sf, sb = mixer(h, w_in_l, lb_l, g_norm_l, w_a_l, w_b_l, w_out_l, s0f, s0b)
    x = x + g1 * m
    h = rmsnorm(x, n2) * (1.0 + sc2) + sh2
    a, gt = jnp.split(h @ w_ff_in_l, 2, axis=-1)
    x = x + g2 * ((jax.nn.silu(a) * gt) @ w_ff_out_l)
    return x, sf, sb


def setup_inputs(seed: int = 0) -> dict:
    key = jax.random.key(seed)
    ks = jax.random.split(key, 20)
    nrm = lambda k, shape, s: jax.random.normal(k, shape, jnp.float32) * s
    return {
        "x_prompt": nrm(ks[0], (BATCH, SEQ, D_MODEL), 1.0),
        "x_sample": nrm(ks[1], (DEC_BATCH, DEC_SEQ, D_MODEL), 1.0),
        "state_hgrn": nrm(ks[2], (DEC_BATCH, DEPTH, 2, H_A, DK, DV), 1.0),
        "c": nrm(ks[3], (DEC_BATCH, D_MODEL), 1.0),
        "c_ctx": nrm(ks[4], (D_MODEL,), 1.0),
        "w_mod": nrm(ks[5], (DEPTH, D_MODEL, 6 * D_MODEL), D_MODEL ** -0.5),
        "b_mod": nrm(ks[6], (DEPTH, 6 * D_MODEL), 0.02),
        "norm_mix": 1.0 + nrm(ks[7], (DEPTH, D_MODEL), 0.02),
        "norm_ffn": 1.0 + nrm(ks[8], (DEPTH, D_MODEL), 0.02),
        "w_in": nrm(ks[9], (DEPTH, D_MODEL, N_IN), D_MODEL ** -0.5),
        "lb_raw": nrm(ks[10], (DEPTH, 2, W_A), 1.0),
        "g_norm": 1.0 + nrm(ks[11], (DEPTH, DV), 0.02),
        "w_a": nrm(ks[12], (DEPTH, W_A, D_MODEL), W_A ** -0.5),
        "w_b": nrm(ks[13], (DEPTH, W_B, D_MODEL), W_B ** -0.5),
        "w_out": nrm(ks[14], (DEPTH, D_MODEL, D_MODEL), D_MODEL ** -0.5),
        "w_ff_in": nrm(ks[15], (DEPTH, D_MODEL, 2 * D_FF), D_MODEL ** -0.5),
        "w_ff_out": nrm(ks[16], (DEPTH, D_FF, D_MODEL), D_FF ** -0.5),
        "norm_final": 1.0 + nrm(ks[17], (D_MODEL,), 0.02),
    }


def reference(x_prompt, x_sample, state_hgrn, c, c_ctx, w_mod, b_mod, norm_mix, norm_ffn, w_in, lb_raw,
              g_norm, w_a, w_b, w_out, w_ff_in, w_ff_out, norm_final):
    lb_all = jnp.cumsum(jax.nn.softmax(lb_raw.astype(jnp.float32), axis=0), axis=0)
    lb_all = lb_all - lb_all[0:1]
    xc = x_prompt
    xs = x_sample + grid_pos_embed(x_sample.shape[1], x_sample.dtype)[None]
    zeros = jnp.zeros((x_prompt.shape[0], H_A, DK, DV), jnp.float32)
    new_states = []
    for l in range(DEPTH):
        mod_ctx = adaln(c_ctx[None, :], w_mod[l], b_mod[l])
        mod_lat = adaln(c, w_mod[l], b_mod[l])
        xc, sf, sb = layer(xc, mod_ctx, norm_mix[l], norm_ffn[l], w_in[l], lb_all[l], g_norm[l], w_a[l], w_b[l],
                           w_out[l], w_ff_in[l], w_ff_out[l], zeros, zeros)
        new_states.append(jnp.stack([sf, sb], axis=1))
        xs, _, _ = layer(xs, mod_lat, norm_mix[l], norm_ffn[l], w_in[l], lb_all[l], g_norm[l], w_a[l], w_b[l],
                         w_out[l], w_ff_in[l], w_ff_out[l],
                         state_hgrn[:, l, 0].astype(jnp.float32), state_hgrn[:, l, 1].astype(jnp.float32))
    state_hgrn_new = jnp.stack(new_states, axis=1).astype(x_prompt.dtype)
    y_prompt = rmsnorm(xc, norm_final)
    y_sample = rmsnorm(xs, norm_final)
    return (y_prompt, y_sample, state_hgrn_new)
```

```python
import functools
import math

import jax
import jax.numpy as jnp
from jax import lax
from jax.experimental import pallas as pl
from jax.experimental.pallas import tpu as pltpu

D_MODEL = 2048
DEPTH = 2
H_A = 8
DK = 128
DV = 128
W_A = H_A * DK
N_FG = 4
FG = 256
W_B = N_FG * FG
D_FF = 5632
N_IN = 5 * W_A + W_B + 2 * D_MODEL
EPS = 1e-6
GRID_W = 64
POS_BASE = 10000.0

COL_Q = 0
COL_FF = W_A
COL_FB = 2 * W_A
COL_IV = 3 * W_A
COL_OG = 4 * W_A
COL_U = 5 * W_A
COL_GA = 5 * W_A + W_B
COL_GB = 5 * W_A + W_B + D_MODEL

MOD_ROWS = 8
LANES = 128
SUBLANES = 8
VMEM_LIMIT = 56 * 1024 * 1024

TM_IN, TN_IN = 512, 1024
TM_MIX = 256
TM_FFN, TN_FFN = 512, 512
TM_CDFT = 512
TK_SDFT = 256
SCAN_ROWS = 128
SCAN_SUB = 16
SCAN_HEADS = 2

F32 = jnp.float32
BF16 = jnp.bfloat16


def _params(sem):
    return pltpu.CompilerParams(dimension_semantics=sem, vmem_limit_bytes=VMEM_LIMIT)


def _silu(x):
    return x * jax.nn.sigmoid(x)


def _rms(x):
    return x * lax.rsqrt(jnp.mean(x * x, axis=-1, keepdims=True) + EPS)


def _adaln_kernel(c_ref, w_ref, b_ref, out_ref):
    s = _silu(c_ref[...]).astype(BF16)
    out_ref[...] = jnp.dot(s, w_ref[...].astype(BF16), preferred_element_type=F32) + b_ref[...]


def adaln_table(cvec, w_mod, b_mod):
    tn = 1024
    out = pl.pallas_call(
        _adaln_kernel,
        grid=(DEPTH, 6 * D_MODEL // tn),
        in_specs=[
            pl.BlockSpec((MOD_ROWS, D_MODEL), lambda l, j: (0, 0)),
            pl.BlockSpec((None, D_MODEL, tn), lambda l, j: (l, 0, j)),
            pl.BlockSpec((None, 1, tn), lambda l, j: (l, 0, j)),
        ],
        out_specs=pl.BlockSpec((None, MOD_ROWS, tn), lambda l, j: (l, 0, j)),
        out_shape=jax.ShapeDtypeStruct((DEPTH, MOD_ROWS, 6 * D_MODEL), F32),
        compiler_params=_params(("parallel", "parallel")),
        name="adaln",
    )(cvec, w_mod, b_mod.reshape(DEPTH, 1, 6 * D_MODEL))
    return out.reshape(DEPTH, MOD_ROWS, 6, D_MODEL)


def _proj_in_kernel(*refs, has_pe):
    if has_pe:
        x_ref, pe_ref, mod_ref, n_ref, w_ref, out_ref, h_ref = refs
    else:
        x_ref, mod_ref, n_ref, w_ref, out_ref, h_ref = refs

    @pl.when(pl.program_id(2) == 0)
    def _():
        x = x_ref[...]
        if has_pe:
            x = x + pe_ref[...]
        h = _rms(x) * n_ref[...]
        h = h * (1.0 + mod_ref[1:2, :]) + mod_ref[0:1, :]
        h_ref[...] = h.astype(BF16)

    out_ref[...] = jnp.dot(h_ref[...], w_ref[...], preferred_element_type=F32)


def proj_in(x, pe, mod, norm_w, w_in_bf, layer, row0):
    bsz, n, _ = x.shape
    tm = min(TM_IN, n)
    has_pe = pe is not None
    in_specs = [pl.BlockSpec((None, tm, D_MODEL), lambda b, i, j: (b, i, 0))]
    args = [x]
    if has_pe:
        in_specs.append(pl.BlockSpec((tm, D_MODEL), lambda b, i, j: (i, 0)))
        args.append(pe)
    in_specs += [
        pl.BlockSpec((None, None, 6, D_MODEL), lambda b, i, j: (layer, row0 + b, 0, 0)),
        pl.BlockSpec((None, 1, D_MODEL), lambda b, i, j: (layer, 0, 0)),
        pl.BlockSpec((None, D_MODEL, TN_IN), lambda b, i, j: (layer, 0, j)),
    ]
    args += [mod, norm_w, w_in_bf]
    return pl.pallas_call(
        functools.partial(_proj_in_kernel, has_pe=has_pe),
        grid=(bsz, n // tm, N_IN // TN_IN),
        in_specs=in_specs,
        out_specs=pl.BlockSpec((None, tm, TN_IN), lambda b, i, j: (b, i, j)),
        out_shape=jax.ShapeDtypeStruct((bsz, n, N_IN), F32),
        scratch_shapes=[pltpu.VMEM((tm, D_MODEL), BF16)],
        compiler_params=_params(("parallel", "parallel", "arbitrary")),
        name="proj_in",
    )(*args)


def _split3(x):
    hi = x.astype(BF16)
    r = x - hi.astype(F32)
    mid = r.astype(BF16)
    lo = (r - mid.astype(F32)).astype(BF16)
    return hi, mid, lo


def _intra(q, k, b, v, reverse):
    nh = SCAN_SUB // SUBLANES
    t_idx = lax.broadcasted_iota(jnp.int32, (SUBLANES, LANES), 0)
    acc = [jnp.zeros((SUBLANES, LANES), F32) for _ in range(nh)]
    for j in range(SCAN_SUB):
        kj, bj, vj = k[j:j + 1], b[j:j + 1], v[j:j + 1]
        for hf in range(nh):
            lo = SUBLANES * hf
            hi = lo + SUBLANES - 1
            if (hi < j) if not reverse else (lo > j):
                continue
            d = b[lo:lo + SUBLANES] - bj
            full = (lo >= j) if not reverse else (hi <= j)
            if not full:
                keep = (t_idx + lo >= j) if not reverse else (t_idx + lo <= j)
                d = jnp.where(keep, d, -1e30)
            p = q[lo:lo + SUBLANES] * kj * jnp.exp(d)
            a = jnp.sum(p, axis=1, keepdims=True)
            acc[hf] = acc[hf] + a * vj
    return jnp.concatenate(acc, axis=0)


def _scan_kernel(*refs, layer, reverse, has_s0):
    if has_s0:
        q_ref, x_ref, v_ref, lb_ref, s0_ref, o_ref, sfin_ref, st_ref = refs
    else:
        q_ref, x_ref, v_ref, lb_ref, o_ref, sfin_ref, st_ref = refs
    i = pl.program_id(2)
    rows = q_ref.shape[0]
    nsub = rows // SCAN_SUB

    @pl.when(i == 0)
    def _():
        for hh in range(SCAN_HEADS):
            if has_s0:
                st_ref[hh] = s0_ref[hh].T
            else:
                st_ref[hh] = jnp.zeros((DV, DK), F32)

    r_idx = lax.broadcasted_iota(jnp.int32, (rows, rows), 0)
    s_idx = lax.broadcasted_iota(jnp.int32, (rows, rows), 1)
    same = (r_idx ^ s_idx) < SCAN_SUB
    order = (s_idx >= r_idx) if reverse else (s_idx <= r_idx)
    tmat = jnp.where(same & order, 1.0, 0.0).astype(BF16)

    lbr = lb_ref[...]
    e = jnp.exp(lbr - jnp.max(lbr, axis=0, keepdims=True))
    lb_all = jnp.sum(e[1:layer + 1], axis=0, keepdims=True) / jnp.sum(e, axis=0, keepdims=True) \
        if layer > 0 else jnp.zeros((1, lbr.shape[1]), F32)

    for hh in range(SCAN_HEADS):
        sl = slice(LANES * hh, LANES * (hh + 1))
        lbv = lb_all[:, sl]
        q = q_ref[:, sl]
        v = v_ref[:, sl]
        f = lbv + (1.0 - lbv) * jax.nn.sigmoid(x_ref[:, sl])
        k = 1.0 - f
        g = jnp.log(f)
        ghi, gmid, glo = _split3(g)
        b = (jnp.dot(tmat, glo, preferred_element_type=F32)
             + jnp.dot(tmat, gmid, preferred_element_type=F32)
             + jnp.dot(tmat, ghi, preferred_element_type=F32))
        st = st_ref[hh]
        blocks = range(nsub - 1, -1, -1) if reverse else range(nsub)
        for blk in blocks:
            r = slice(SCAN_SUB * blk, SCAN_SUB * (blk + 1))
            qb, kb, bb, vb = q[r], k[r], b[r], v[r]
            b_end = bb[0:1] if reverse else bb[SCAN_SUB - 1:SCAN_SUB]
            qt = (qb * jnp.exp(bb)).astype(BF16)
            o = lax.dot_general(qt, st.astype(BF16), (((1,), (1,)), ((), ())),
                                preferred_element_type=F32)
            o = o + _intra(qb, kb, bb, vb, reverse)
            o_ref[r, sl] = o
            khat = (kb * jnp.exp(b_end - bb)).astype(BF16)
            st = st * jnp.exp(b_end) + lax.dot_general(
                vb.astype(BF16), khat, (((0,), (0,)), ((), ())), preferred_element_type=F32)
        st_ref[hh] = st

    @pl.when(i == pl.num_programs(2) - 1)
    def _():
        for hh in range(SCAN_HEADS):
            sfin_ref[hh] = st_ref[hh].T


def scan(p, lb_dir, state, layer, direction):
    bsz, n, _ = p.shape
    reverse = direction == 1
    nblk = n // SCAN_ROWS
    wblk = LANES * SCAN_HEADS
    has_s0 = state is not None

    def rowblk(i):
        return nblk - 1 - i if reverse else i

    def col(c0):
        return lambda b, h, i: (b, rowblk(i), c0 // wblk + h)

    in_specs = [
        pl.BlockSpec((None, SCAN_ROWS, wblk), col(COL_Q)),
        pl.BlockSpec((None, SCAN_ROWS, wblk), col(COL_FB if reverse else COL_FF)),
        pl.BlockSpec((None, SCAN_ROWS, wblk), col(COL_IV)),
        pl.BlockSpec((None, DEPTH, wblk), lambda b, h, i: (direction, 0, h)),
    ]
    args = [p, p, p, lb_dir]
    if has_s0:
        in_specs.append(pl.BlockSpec((None, None, None, SCAN_HEADS, DK, DV),
                                     lambda b, h, i: (b, layer, direction, h, 0, 0)))
        args.append(state)
    return pl.pallas_call(
        functools.partial(_scan_kernel, layer=layer, reverse=reverse, has_s0=has_s0),
        grid=(bsz, H_A // SCAN_HEADS, nblk),
        in_specs=in_specs,
        out_specs=[
            pl.BlockSpec((None, SCAN_ROWS, wblk), lambda b, h, i: (b, rowblk(i), h)),
            pl.BlockSpec((None, SCAN_HEADS, DK, DV), lambda b, h, i: (b, h, 0, 0)),
        ],
        out_shape=[
            jax.ShapeDtypeStruct((bsz, n, W_A), F32),
            jax.ShapeDtypeStruct((bsz, H_A, DK, DV), F32),
        ],
        scratch_shapes=[pltpu.VMEM((SCAN_HEADS, DV, DK), F32)],
        compiler_params=_params(("parallel", "parallel", "arbitrary")),
        name="scan_bwd" if reverse else "scan_fwd",
    )(*args)


def _chan_dft_kernel(u_ref, cs_ref, a_ref, b_ref):
    cs = cs_ref[...]
    for g in range(N_FG):
        sl = slice(FG * g, FG * (g + 1))
        r = jnp.dot(u_ref[:, sl].astype(BF16), cs, preferred_element_type=F32)
        a_ref[:, sl] = r[:, :FG].astype(BF16)
        b_ref[:, sl] = r[:, FG:].astype(BF16)


def chan_dft(p, cs_f):
    bsz, n, _ = p.shape
    tm = min(TM_CDFT, n)
    out = jax.ShapeDtypeStruct((bsz, n, W_B), BF16)
    spec = pl.BlockSpec((None, tm, W_B), lambda b, i: (b, i, 0))
    return pl.pallas_call(
        _chan_dft_kernel,
        grid=(bsz, n // tm),
        in_specs=[
            pl.BlockSpec((None, tm, W_B), lambda b, i: (b, i, COL_U // W_B)),
            pl.BlockSpec((FG, 2 * FG), lambda b, i: (0, 0)),
        ],
        out_specs=[spec, spec],
        out_shape=[out, out],
        compiler_params=_params(("parallel", "parallel")),
        name="chan_dft",
    )(p, cs_f)


def _seq_dft_kernel(c_ref, s_ref, a_ref, b_ref, z_ref, *, scale):
    acc = jnp.dot(c_ref[...], a_ref[...], preferred_element_type=F32)
    acc = acc - jnp.dot(s_ref[...], b_ref[...], preferred_element_type=F32)
    z_ref[...] = (acc * scale).astype(BF16)


def seq_dft(a, b, cos_n, sin_n):
    bsz, n, _ = a.shape
    tk = min(TK_SDFT, n)
    full = pl.BlockSpec((None, n, W_B), lambda bi, i: (bi, 0, 0))
    slab = pl.BlockSpec((tk, n), lambda bi, i: (i, 0))
    return pl.pallas_call(
        functools.partial(_seq_dft_kernel, scale=1.0 / math.sqrt(n * FG)),
        grid=(bsz, n // tk),
        in_specs=[slab, slab, full, full],
        out_specs=pl.BlockSpec((None, tk, W_B), lambda bi, i: (bi, i, 0)),
        out_shape=jax.ShapeDtypeStruct((bsz, n, W_B), BF16),
        compiler_params=_params(("parallel", "parallel")),
        name="seq_dft",
    )(cos_n, sin_n, a, b)


def dft_tables(n):
    idx = jnp.arange(n, dtype=jnp.int32)
    ang = ((idx[:, None] * idx[None, :]) % n).astype(F32) * (2.0 * math.pi / n)
    return jnp.cos(ang).astype(BF16), jnp.sin(ang).astype(BF16)


def _mix_out_kernel(*refs, has_pe):
    if has_pe:
        (of_ref, ob_ref, og_ref, z_ref, ga_ref, gb_ref, x_ref, pe_ref, mod_ref, gn_ref,
         wa_ref, wb_ref, wo_ref, out_ref) = refs
    else:
        (of_ref, ob_ref, og_ref, z_ref, ga_ref, gb_ref, x_ref, mod_ref, gn_ref,
         wa_ref, wb_ref, wo_ref, out_ref) = refs
    gn = gn_ref[...]
    parts = []
    for h in range(H_A):
        sl = slice(DV * h, DV * (h + 1))
        oh = _rms(of_ref[:, sl] + ob_ref[:, sl]) * gn
        parts.append((oh * _silu(og_ref[:, sl])).astype(BF16))
    o = jnp.concatenate(parts, axis=1)
    ya = jnp.dot(o, wa_ref[...], preferred_element_type=F32)
    yb = jnp.dot(z_ref[...], wb_ref[...], preferred_element_type=F32)
    merged = jax.nn.sigmoid(ga_ref[...]) * ya + jax.nn.sigmoid(gb_ref[...]) * yb
    m = jnp.dot(merged.astype(BF16), wo_ref[...], preferred_element_type=F32)
    x = x_ref[...]
    if has_pe:
        x = x + pe_ref[...]
    out_ref[...] = x + mod_ref[2:3, :] * m


def mix_out(o_f, o_b, p, z, x, pe, mod, g_norm, w_a_bf, w_b_bf, w_out_bf, layer, row0):
    bsz, n, _ = x.shape
    tm = min(TM_MIX, n)
    has_pe = pe is not None

    def tok(width, cblk=0):
        return pl.BlockSpec((None, tm, width), lambda b, i: (b, i, cblk))

    def whole(rows, cols):
        return pl.BlockSpec((None, rows, cols), lambda b, i: (layer, 0, 0),
                            pipeline_mode=pl.Buffered(1))

    in_specs = [tok(W_A), tok(W_A), tok(W_A, COL_OG // W_A), tok(W_B),
                tok(D_MODEL, COL_GA // D_MODEL), tok(D_MODEL, COL_GB // D_MODEL), tok(D_MODEL)]
    args = [o_f, o_b, p, z, p, p, x]
    if has_pe:
        in_specs.append(pl.BlockSpec((tm, D_MODEL), lambda b, i: (i, 0)))
        args.append(pe)
    in_specs += [
        pl.BlockSpec((None, None, 6, D_MODEL), lambda b, i: (layer, row0 + b, 0, 0)),
        pl.BlockSpec((None, 1, DV), lambda b, i: (layer, 0, 0)),
        whole(W_A, D_MODEL), whole(W_B, D_MODEL), whole(D_MODEL, D_MODEL),
    ]
    args += [mod, g_norm, w_a_bf, w_b_bf, w_out_bf]
    return pl.pallas_call(
        functools.partial(_mix_out_kernel, has_pe=has_pe),
        grid=(bsz, n // tm),
        in_specs=in_specs,
        out_specs=tok(D_MODEL),
        out_shape=jax.ShapeDtypeStruct((bsz, n, D_MODEL), F32),
        compiler_params=_params(("parallel", "parallel")),
        name="mix_out",
    )(*args)


def _ffn_kernel(*refs, has_final):
    if has_final:
        x_ref, mod_ref, n_ref, w1a_ref, w1g_ref, w2_ref, nf_ref, out_ref, h_ref, acc_ref = refs
    else:
        x_ref, mod_ref, n_ref, w1a_ref, w1g_ref, w2_ref, out_ref, h_ref, acc_ref = refs
    j = pl.program_id(2)

    @pl.when(j == 0)
    def _():
        h = _rms(x_ref[...]) * n_ref[...]
        h = h * (1.0 + mod_ref[4:5, :]) + mod_ref[3:4, :]
        h_ref[...] = h.astype(BF16)
        acc_ref[...] = jnp.zeros_like(acc_ref)

    h = h_ref[...]
    a = jnp.dot(h, w1a_ref[...], preferred_element_type=F32)
    gt = jnp.dot(h, w1g_ref[...], preferred_element_type=F32)
    act = (_silu(a) * gt).astype(BF16)
    acc_ref[...] += jnp.dot(act, w2_ref[...], preferred_element_type=F32)

    @pl.when(j == pl.num_programs(2) - 1)
    def _():
        y = x_ref[...] + mod_ref[5:6, :] * acc_ref[...]
        if has_final:
            y = _rms(y) * nf_ref[...]
        out_ref[...] = y


def ffn(x, mod, norm_w, w_ff_in_bf, w_ff_out_bf, norm_final, layer, row0):
    bsz, n, _ = x.shape
    tm = min(TM_FFN, n)
    nj = D_FF // TN_FFN
    has_final = norm_final is not None
    in_specs = [
        pl.BlockSpec((None, tm, D_MODEL), lambda b, i, j: (b, i, 0)),
        pl.BlockSpec((None, None, 6, D_MODEL), lambda b, i, j: (layer, row0 + b, 0, 0)),
        pl.BlockSpec((None, 1, D_MODEL), lambda b, i, j: (layer, 0, 0)),
        pl.BlockSpec((None, D_MODEL, TN_FFN), lambda b, i, j: (layer, 0, j)),
        pl.BlockSpec((None, D_MODEL, TN_FFN), lambda b, i, j: (layer, 0, nj + j)),
        pl.BlockSpec((None, TN_FFN, D_MODEL), lambda b, i, j: (layer, j, 0)),
    ]
    args = [x, mod, norm_w, w_ff_in_bf, w_ff_in_bf, w_ff_out_bf]
    if has_final:
        in_specs.append(pl.BlockSpec((1, D_MODEL), lambda b, i, j: (0, 0)))
        args.append(norm_final)
    return pl.pallas_call(
        functools.partial(_ffn_kernel, has_final=has_final),
        grid=(bsz, n // tm, nj),
        in_specs=in_specs,
        out_specs=pl.BlockSpec((None, tm, D_MODEL), lambda b, i, j: (b, i, 0)),
        out_shape=jax.ShapeDtypeStruct((bsz, n, D_MODEL), F32),
        scratch_shapes=[pltpu.VMEM((tm, D_MODEL), BF16), pltpu.VMEM((tm, D_MODEL), F32)],
        compiler_params=_params(("parallel", "parallel", "arbitrary")),
        name="ffn",
    )(*args)


def grid_pos_embed(n):
    rows = n // GRID_W
    r, col = jnp.meshgrid(jnp.arange(rows, dtype=F32), jnp.arange(GRID_W, dtype=F32), indexing="ij")
    quarter = D_MODEL // 4
    omega = 1.0 / (POS_BASE ** (jnp.arange(quarter, dtype=F32) / quarter))

    def enc(pos):
        a = pos.reshape(-1)[:, None] * omega[None, :]
        return jnp.concatenate([jnp.sin(a), jnp.cos(a)], axis=-1)

    return jnp.concatenate([enc(r), enc(col)], axis=-1)


def kernel(x_prompt, x_sample, state_hgrn, c, c_ctx, w_mod, b_mod, norm_mix, norm_ffn, w_in, lb_raw,
           g_norm, w_a, w_b, w_out, w_ff_in, w_ff_out, norm_final):
    bc, nc, _ = x_prompt.shape
    bl, nl, _ = x_sample.shape
    assert 1 + bl <= MOD_ROWS

    cvec = jnp.zeros((MOD_ROWS, D_MODEL), F32).at[0].set(c_ctx).at[1:1 + bl].set(c)
    mod = adaln_table(cvec, w_mod, b_mod)

    w_in_bf, w_a_bf, w_b_bf, w_out_bf = (w.astype(BF16) for w in (w_in, w_a, w_b, w_out))
    w_ff_in_bf, w_ff_out_bf = w_ff_in.astype(BF16), w_ff_out.astype(BF16)
    norm_mix3 = norm_mix.reshape(DEPTH, 1, D_MODEL)
    norm_ffn3 = norm_ffn.reshape(DEPTH, 1, D_MODEL)
    g_norm3 = g_norm.reshape(DEPTH, 1, DV)
    norm_final2 = norm_final.reshape(1, D_MODEL)
    lb_dir = jnp.transpose(lb_raw, (1, 0, 2))

    fidx = jnp.arange(FG, dtype=jnp.int32)
    fang = ((fidx[:, None] * fidx[None, :]) % FG).astype(F32) * (2.0 * math.pi / FG)
    cs_f = jnp.concatenate([jnp.cos(fang), jnp.sin(fang)], axis=1).astype(BF16)
    tables = {n: dft_tables(n) for n in {nc, nl}}
    pe = grid_pos_embed(nl)

    def layer(x, seq_shape, pe_l, state, l, row0):
        bt, nt, _ = x.shape
        bs, ns = seq_shape
        p = proj_in(x, pe_l, mod, norm_mix3, w_in_bf, l, row0)
        ps = p.reshape(bs, ns, N_IN)
        o_f, s_f = scan(ps, lb_dir, state, l, 0)
        o_b, s_b = scan(ps, lb_dir, state, l, 1)
        fa, fb = chan_dft(ps, cs_f)
        z = seq_dft(fa, fb, *tables[ns])
        x = mix_out(o_f.reshape(bt, nt, W_A), o_b.reshape(bt, nt, W_A), p, z.reshape(bt, nt, W_B),
                    x, pe_l, mod, g_norm3, w_a_bf, w_b_bf, w_out_bf, l, row0)
        x = ffn(x, mod, norm_ffn3, w_ff_in_bf, w_ff_out_bf,
                norm_final2 if l == DEPTH - 1 else None, l, row0)
        return x, s_f, s_b

    xc = x_prompt.reshape(1, bc * nc, D_MODEL)
    xs = x_sample
    new_states = []
    for l in range(DEPTH):
        xc, s_f, s_b = layer(xc, (bc, nc), None, None, l, 0)
        new_states.append(jnp.stack([s_f, s_b], axis=1))
        xs, _, _ = layer(xs, (bl, nl), pe if l == 0 else None, state_hgrn, l, 1)
    state_new = jnp.stack(new_states, axis=1).astype(x_prompt.dtype)
    return xc.reshape(bc, nc, D_MODEL), xs, state_new
```

```python
import functools
import math

import jax
import jax.numpy as jnp
from jax import lax
from jax.experimental import pallas as pl
from jax.experimental.pallas import tpu as pltpu

D_MODEL = 2048
DEPTH = 2
H_A = 8
DK = 128
DV = 128
W_A = H_A * DK
N_FG = 4
FG = 256
W_B = N_FG * FG
D_FF = 5632
N_IN = 5 * W_A + W_B + 2 * D_MODEL
EPS = 1e-6
GRID_W = 64
POS_BASE = 10000.0

COL_Q = 0
COL_FF = W_A
COL_FB = 2 * W_A
COL_IV = 3 * W_A
COL_OG = 4 * W_A
COL_U = 5 * W_A
COL_GA = 5 * W_A + W_B
COL_GB = 5 * W_A + W_B + D_MODEL

MOD_ROWS = 8
LANES = 128
SUBLANES = 8
VMEM_LIMIT = 56 * 1024 * 1024

TM_IN, TN_IN, TN_IN_PE = 1024, 1024, 512
TM_MIX = 256
TM_FFN, TN_FFN = 1024, 256
TM_CDFT = 512
TK_SDFT = 256
SCAN_ROWS = 256
SCAN_CHUNK = 128
SCAN_LEVELS = (64, 32, 16, 8)
SCAN_HEADS = 4
LOG2E = 1.4426950408889634

F32 = jnp.float32
BF16 = jnp.bfloat16


def _params(sem):
    return pltpu.CompilerParams(dimension_semantics=sem, vmem_limit_bytes=VMEM_LIMIT)


def _silu(x):
    return x * jax.nn.sigmoid(x)


def _rms(x):
    return x * lax.rsqrt(jnp.mean(x * x, axis=-1, keepdims=True) + EPS)


def _adaln_kernel(c_ref, w_ref, b_ref, out_ref):
    s = _silu(c_ref[...]).astype(BF16)
    out_ref[...] = jnp.dot(s, w_ref[...].astype(BF16), preferred_element_type=F32) + b_ref[...]


def adaln_table(cvec, w_mod, b_mod):
    tn = 1024
    out = pl.pallas_call(
        _adaln_kernel,
        grid=(DEPTH, 6 * D_MODEL // tn),
        in_specs=[
            pl.BlockSpec((MOD_ROWS, D_MODEL), lambda l, j: (0, 0)),
            pl.BlockSpec((None, D_MODEL, tn), lambda l, j: (l, 0, j)),
            pl.BlockSpec((None, 1, tn), lambda l, j: (l, 0, j)),
        ],
        out_specs=pl.BlockSpec((None, MOD_ROWS, tn), lambda l, j: (l, 0, j)),
        out_shape=jax.ShapeDtypeStruct((DEPTH, MOD_ROWS, 6 * D_MODEL), F32),
        compiler_params=_params(("parallel", "parallel")),
        name="adaln",
    )(cvec, w_mod, b_mod.reshape(DEPTH, 1, 6 * D_MODEL))
    return out.reshape(DEPTH, MOD_ROWS, 6, D_MODEL)


def _proj_in_kernel(*refs, has_pe):
    if has_pe:
        x_ref, pe_ref, mod_ref, n_ref, w_ref, out_ref, h_ref = refs
    else:
        x_ref, mod_ref, n_ref, w_ref, out_ref, h_ref = refs

    @pl.when(pl.program_id(2) == 0)
    def _():
        x = x_ref[...]
        if has_pe:
            x = x + pe_ref[...]
        h = _rms(x) * n_ref[...]
        h = h * (1.0 + mod_ref[1:2, :]) + mod_ref[0:1, :]
        h_ref[...] = h.astype(BF16)

    out_ref[...] = jnp.dot(h_ref[...], w_ref[...], preferred_element_type=F32)


def proj_in(x, pe, mod, norm_w, w_in_bf, layer, row0):
    bsz, n, _ = x.shape
    tm = min(TM_IN, n)
    has_pe = pe is not None
    tn = TN_IN_PE if has_pe else TN_IN
    in_specs = [pl.BlockSpec((None, tm, D_MODEL), lambda b, i, j: (b, i, 0))]
    args = [x]
    if has_pe:
        in_specs.append(pl.BlockSpec((tm, D_MODEL), lambda b, i, j: (i, 0)))
        args.append(pe)
    in_specs += [
        pl.BlockSpec((None, None, 6, D_MODEL), lambda b, i, j: (layer, row0 + b, 0, 0)),
        pl.BlockSpec((None, 1, D_MODEL), lambda b, i, j: (layer, 0, 0)),
        pl.BlockSpec((None, D_MODEL, tn), lambda b, i, j: (layer, 0, j)),
    ]
    args += [mod, norm_w, w_in_bf]
    return pl.pallas_call(
        functools.partial(_proj_in_kernel, has_pe=has_pe),
        grid=(bsz, n // tm, N_IN // tn),
        in_specs=in_specs,
        out_specs=pl.BlockSpec((None, tm, tn), lambda b, i, j: (b, i, j)),
        out_shape=jax.ShapeDtypeStruct((bsz, n, N_IN), F32),
        scratch_shapes=[pltpu.VMEM((tm, D_MODEL), BF16)],
        compiler_params=_params(("parallel", "parallel", "arbitrary")),
        name="proj_in",
    )(*args)


def _nt(a, b):
    return lax.dot_general(a, b, (((1,), (1,)), ((), ())), preferred_element_type=F32)


def _tn(a, b):
    return lax.dot_general(a, b, (((0,), (0,)), ((), ())), preferred_element_type=F32)


def _chunk(q_ref, k_ref, b_ref, v_ref, base, sl, st, reverse, level_masks, row_masks):
    def rows(ref, r0, n):
        return ref[pl.ds(base + r0, n), sl]

    def row(ref, r):
        return ref[pl.ds(base + r, 1), sl]

    q = rows(q_ref, 0, SCAN_CHUNK)
    k = rows(k_ref, 0, SCAN_CHUNK)
    b = rows(b_ref, 0, SCAN_CHUNK)
    v = rows(v_ref, 0, SCAN_CHUNK)
    vb = v.astype(BF16)
    b_end = row(b_ref, 0 if reverse else SCAN_CHUNK - 1)

    o = _nt((q * jnp.exp2(b)).astype(BF16), st.astype(BF16))

    a = jnp.zeros((SCAN_CHUNK, SCAN_CHUNK), F32)
    for h, m in zip(SCAN_LEVELS, level_masks):
        z = jnp.zeros((h, LANES), F32)
        qp, kp = [], []
        for r0 in range(0, SCAN_CHUNK, 2 * h):
            if reverse:
                beta = row(b_ref, r0 + h)
                qp += [rows(q_ref, r0, h) * jnp.exp2(rows(b_ref, r0, h) - beta), z]
                kp += [z, rows(k_ref, r0 + h, h) * jnp.exp2(beta - rows(b_ref, r0 + h, h))]
            else:
                beta = row(b_ref, r0 + h - 1)
                kp += [rows(k_ref, r0, h) * jnp.exp2(beta - rows(b_ref, r0, h)), z]
                qp += [z, rows(q_ref, r0 + h, h) * jnp.exp2(rows(b_ref, r0 + h, h) - beta)]
        p = _nt(jnp.concatenate(qp, axis=0).astype(BF16), jnp.concatenate(kp, axis=0).astype(BF16))
        a = jnp.where(m, p, a)
    o = o + jnp.dot(a.astype(BF16), vb, preferred_element_type=F32)

    parts = []
    for r0 in range(0, SCAN_CHUNK, SUBLANES):
        qv = rows(q_ref, r0, SUBLANES)
        bv = rows(b_ref, r0, SUBLANES)
        acc = jnp.zeros((SUBLANES, LANES), F32)
        for j in range(SUBLANES):
            shape = (SUBLANES, LANES)
            kj = jnp.broadcast_to(row(k_ref, r0 + j), shape)
            bj = jnp.broadcast_to(row(b_ref, r0 + j), shape)
            vj = jnp.broadcast_to(row(v_ref, r0 + j), shape)
            d = bv - bj
            if row_masks[j] is not None:
                d = jnp.where(row_masks[j], d, -1e30)
            pr = qv * kj * jnp.exp2(d)
            acc = acc + jnp.sum(pr, axis=1, keepdims=True) * vj
        parts.append(acc)
    o = o + jnp.concatenate(parts, axis=0)

    khat = (k * jnp.exp2(b_end - b)).astype(BF16)
    st = st * jnp.exp2(b_end) + _tn(vb, khat)
    return o, st


def _scan_kernel(*refs, layer, reverse, has_s0):
    if has_s0:
        q_ref, x_ref, v_ref, lb_ref, s0_ref, o_ref, sfin_ref, st_ref, k_scr, b_scr = refs
    else:
        q_ref, x_ref, v_ref, lb_ref, o_ref, sfin_ref, st_ref, k_scr, b_scr = refs
    i = pl.program_id(2)
    nch = q_ref.shape[0] // SCAN_CHUNK
    width = q_ref.shape[1]

    @pl.when(i == 0)
    def _():
        for hh in range(SCAN_HEADS):
            if has_s0:
                st_ref[hh] = s0_ref[hh].T
            else:
                st_ref[hh] = jnp.zeros((DV, DK), F32)

    t_idx = lax.broadcasted_iota(jnp.int32, (SCAN_CHUNK, SCAN_CHUNK), 0)
    s_idx = lax.broadcasted_iota(jnp.int32, (SCAN_CHUNK, SCAN_CHUNK), 1)
    incl = (s_idx >= t_idx) if reverse else (s_idx <= t_idx)
    strict = (s_idx > t_idx) if reverse else (s_idx < t_idx)
    tmat = jnp.where(incl, 1.0, 0.0).astype(BF16)
    diff = t_idx ^ s_idx
    level_masks = [(diff >= h) & (diff < 2 * h) & strict for h in SCAN_LEVELS]
    sub = lax.broadcasted_iota(jnp.int32, (SUBLANES, LANES), 0)
    if reverse:
        row_masks = [(sub <= j) if j < SUBLANES - 1 else None for j in range(SUBLANES)]
    else:
        row_masks = [(sub >= j) if j > 0 else None for j in range(SUBLANES)]

    lbr = lb_ref[...]
    e = jnp.exp(lbr - jnp.max(lbr, axis=0, keepdims=True))
    lb_all = jnp.sum(e[1:layer + 1], axis=0, keepdims=True) / jnp.sum(e, axis=0, keepdims=True) \
        if layer > 0 else jnp.zeros((1, width), F32)

    for c in (range(nch - 1, -1, -1) if reverse else range(nch)):
        base = SCAN_CHUNK * c
        rs = pl.ds(base, SCAN_CHUNK)
        f = lb_all + (1.0 - lb_all) * jax.nn.sigmoid(x_ref[rs, :])
        g = jnp.log(f) * LOG2E
        ghi = g.astype(BF16)
        glo = (g - ghi.astype(F32)).astype(BF16)
        b2 = jnp.dot(tmat, jnp.concatenate([ghi, glo], axis=1), preferred_element_type=F32)
        k_scr[rs, :] = 1.0 - f
        b_scr[rs, :] = b2[:, :width] + b2[:, width:]
        for hh in range(SCAN_HEADS):
            sl = slice(LANES * hh, LANES * (hh + 1))
            o, st = _chunk(q_ref, k_scr, b_scr, v_ref, base, sl, st_ref[hh], reverse,
                           level_masks, row_masks)
            o_ref[rs, sl] = o
            st_ref[hh] = st

    @pl.when(i == pl.num_programs(2) - 1)
    def _():
        for hh in range(SCAN_HEADS):
            sfin_ref[hh] = st_ref[hh].T


def scan(p, lb_dir, state, layer, direction):
    bsz, n, _ = p.shape
    reverse = direction == 1
    nblk = n // SCAN_ROWS
    wblk = LANES * SCAN_HEADS
    has_s0 = state is not None

    def rowblk(i):
        return nblk - 1 - i if reverse else i

    def col(c0):
        return lambda b, h, i: (b, rowblk(i), c0 // wblk + h)

    in_specs = [
        pl.BlockSpec((None, SCAN_ROWS, wblk), col(COL_Q)),
        pl.BlockSpec((None, SCAN_ROWS, wblk), col(COL_FB if reverse else COL_FF)),
        pl.BlockSpec((None, SCAN_ROWS, wblk), col(COL_IV)),
        pl.BlockSpec((None, DEPTH, wblk), lambda b, h, i: (direction, 0, h)),
    ]
    args = [p, p, p, lb_dir]
    if has_s0:
        in_specs.append(pl.BlockSpec((None, None, None, SCAN_HEADS, DK, DV),
                                     lambda b, h, i: (b, layer, direction, h, 0, 0)))
        args.append(state)
    return pl.pallas_call(
        functools.partial(_scan_kernel, layer=layer, reverse=reverse, has_s0=has_s0),
        grid=(bsz, H_A // SCAN_HEADS, nblk),
        in_specs=in_specs,
        out_specs=[
            pl.BlockSpec((None, SCAN_ROWS, wblk), lambda b, h, i: (b, rowblk(i), h)),
            pl.BlockSpec((None, SCAN_HEADS, DK, DV), lambda b, h, i: (b, h, 0, 0)),
        ],
        out_shape=[
            jax.ShapeDtypeStruct((bsz, n, W_A), F32),
            jax.ShapeDtypeStruct((bsz, H_A, DK, DV), F32),
        ],
        scratch_shapes=[pltpu.VMEM((SCAN_HEADS, DV, DK), F32),
                        pltpu.VMEM((SCAN_ROWS, wblk), F32),
                        pltpu.VMEM((SCAN_ROWS, wblk), F32)],
        compiler_params=_params(("parallel", "parallel", "arbitrary")),
        name="scan_bwd" if reverse else "scan_fwd",
    )(*args)


def _chan_dft_kernel(u_ref, cs_ref, a_ref, b_ref):
    cs = cs_ref[...]
    for g in range(N_FG):
        sl = slice(FG * g, FG * (g + 1))
        r = jnp.dot(u_ref[:, sl].astype(BF16), cs, preferred_element_type=F32)
        a_ref[:, sl] = r[:, :FG].astype(BF16)
        b_ref[:, sl] = r[:, FG:].astype(BF16)


def chan_dft(p, cs_f):
    bsz, n, _ = p.shape
    tm = min(TM_CDFT, n)
    out = jax.ShapeDtypeStruct((bsz, n, W_B), BF16)
    spec = pl.BlockSpec((None, tm, W_B), lambda b, i: (b, i, 0))
    return pl.pallas_call(
        _chan_dft_kernel,
        grid=(bsz, n // tm),
        in_specs=[
            pl.BlockSpec((None, tm, W_B), lambda b, i: (b, i, COL_U // W_B)),
            pl.BlockSpec((FG, 2 * FG), lambda b, i: (0, 0)),
        ],
        out_specs=[spec, spec],
        out_shape=[out, out],
        compiler_params=_params(("parallel", "parallel")),
        name="chan_dft",
    )(p, cs_f)


def _seq_dft_kernel(c_ref, s_ref, a_ref, b_ref, z_ref, *, scale):
    acc = jnp.dot(c_ref[...], a_ref[...], preferred_element_type=F32)
    acc = acc - jnp.dot(s_ref[...], b_ref[...], preferred_element_type=F32)
    z_ref[...] = (acc * scale).astype(BF16)


def seq_dft(a, b, cos_n, sin_n):
    bsz, n, _ = a.shape
    tk = min(TK_SDFT, n)
    full = pl.BlockSpec((None, n, W_B), lambda bi, i: (bi, 0, 0))
    slab = pl.BlockSpec((tk, n), lambda bi, i: (i, 0))
    return pl.pallas_call(
        functools.partial(_seq_dft_kernel, scale=1.0 / math.sqrt(n * FG)),
        grid=(bsz, n // tk),
        in_specs=[slab, slab, full, full],
        out_specs=pl.BlockSpec((None, tk, W_B), lambda bi, i: (bi, i, 0)),
        out_shape=jax.ShapeDtypeStruct((bsz, n, W_B), BF16),
        compiler_params=_params(("parallel", "parallel")),
        name="seq_dft",
    )(cos_n, sin_n, a, b)


def dft_tables(n):
    idx = jnp.arange(n, dtype=jnp.int32)
    ang = ((idx[:, None] * idx[None, :]) % n).astype(F32) * (2.0 * math.pi / n)
    return jnp.cos(ang).astype(BF16), jnp.sin(ang).astype(BF16)


def _mix_out_kernel(*refs, has_pe):
    if has_pe:
        (of_ref, ob_ref, og_ref, z_ref, ga_ref, gb_ref, x_ref, pe_ref, mod_ref, gn_ref,
         wa_ref, wb_ref, wo_ref, out_ref) = refs
    else:
        (of_ref, ob_ref, og_ref, z_ref, ga_ref, gb_ref, x_ref, mod_ref, gn_ref,
         wa_ref, wb_ref, wo_ref, out_ref) = refs
    gn = gn_ref[...]
    parts = []
    for h in range(H_A):
        sl = slice(DV * h, DV * (h + 1))
        oh = _rms(of_ref[:, sl] + ob_ref[:, sl]) * gn
        parts.append((oh * _silu(og_ref[:, sl])).astype(BF16))
    o = jnp.concatenate(parts, axis=1)
    ya = jnp.dot(o, wa_ref[...], preferred_element_type=F32)
    yb = jnp.dot(z_ref[...], wb_ref[...], preferred_element_type=F32)
    merged = jax.nn.sigmoid(ga_ref[...]) * ya + jax.nn.sigmoid(gb_ref[...]) * yb
    m = jnp.dot(merged.astype(BF16), wo_ref[...], preferred_element_type=F32)
    x = x_ref[...]
    if has_pe:
        x = x + pe_ref[...]
    out_ref[...] = x + mod_ref[2:3, :] * m


def mix_out(o_f, o_b, p, z, x, pe, mod, g_norm, w_a_bf, w_b_bf, w_out_bf, layer, row0):
    bsz, n, _ = x.shape
    tm = min(TM_MIX, n)
    has_pe = pe is not None

    def tok(width, cblk=0):
        return pl.BlockSpec((None, tm, width), lambda b, i: (b, i, cblk))

    def whole(rows, cols):
        return pl.BlockSpec((None, rows, cols), lambda b, i: (layer, 0, 0),
                            pipeline_mode=pl.Buffered(1))

    in_specs = [tok(W_A), tok(W_A), tok(W_A, COL_OG // W_A), tok(W_B),
                tok(D_MODEL, COL_GA // D_MODEL), tok(D_MODEL, COL_GB // D_MODEL), tok(D_MODEL)]
    args = [o_f, o_b, p, z, p, p, x]
    if has_pe:
        in_specs.append(pl.BlockSpec((tm, D_MODEL), lambda b, i: (i, 0)))
        args.append(pe)
    in_specs += [
        pl.BlockSpec((None, None, 6, D_MODEL), lambda b, i: (layer, row0 + b, 0, 0)),
        pl.BlockSpec((None, 1, DV), lambda b, i: (layer, 0, 0)),
        whole(W_A, D_MODEL), whole(W_B, D_MODEL), whole(D_MODEL, D_MODEL),
    ]
    args += [mod, g_norm, w_a_bf, w_b_bf, w_out_bf]
    return pl.pallas_call(
        functools.partial(_mix_out_kernel, has_pe=has_pe),
        grid=(bsz, n // tm),
        in_specs=in_specs,
        out_specs=tok(D_MODEL),
        out_shape=jax.ShapeDtypeStruct((bsz, n, D_MODEL), F32),
        compiler_params=_params(("parallel", "parallel")),
        name="mix_out",
    )(*args)


def _ffn_kernel(*refs, has_final):
    if has_final:
        x_ref, mod_ref, n_ref, w1a_ref, w1g_ref, w2_ref, nf_ref, out_ref, h_ref = refs
    else:
        x_ref, mod_ref, n_ref, w1a_ref, w1g_ref, w2_ref, out_ref, h_ref = refs
    j = pl.program_id(2)

    @pl.when(j == 0)
    def _():
        h = _rms(x_ref[...]) * n_ref[...]
        h = h * (1.0 + mod_ref[4:5, :]) + mod_ref[3:4, :]
        h_ref[...] = h.astype(BF16)
        out_ref[...] = jnp.zeros_like(out_ref)

    h = h_ref[...]
    a = jnp.dot(h, w1a_ref[...], preferred_element_type=F32)
    gt = jnp.dot(h, w1g_ref[...], preferred_element_type=F32)
    act = (_silu(a) * gt).astype(BF16)
    out_ref[...] += jnp.dot(act, w2_ref[...], preferred_element_type=F32)

    @pl.when(j == pl.num_programs(2) - 1)
    def _():
        y = x_ref[...] + mod_ref[5:6, :] * out_ref[...]
        if has_final:
            y = _rms(y) * nf_ref[...]
        out_ref[...] = y


def ffn(x, mod, norm_w, w_ff_in_bf, w_ff_out_bf, norm_final, layer, row0):
    bsz, n, _ = x.shape
    tm = min(TM_FFN, n)
    nj = D_FF // TN_FFN
    has_final = norm_final is not None
    in_specs = [
        pl.BlockSpec((None, tm, D_MODEL), lambda b, i, j: (b, i, 0)),
        pl.BlockSpec((None, None, 6, D_MODEL), lambda b, i, j: (layer, row0 + b, 0, 0)),
        pl.BlockSpec((None, 1, D_MODEL), lambda b, i, j: (layer, 0, 0)),
        pl.BlockSpec((None, D_MODEL, TN_FFN), lambda b, i, j: (layer, 0, j)),
        pl.BlockSpec((None, D_MODEL, TN_FFN), lambda b, i, j: (layer, 0, nj + j)),
        pl.BlockSpec((None, TN_FFN, D_MODEL), lambda b, i, j: (layer, j, 0)),
    ]
    args = [x, mod, norm_w, w_ff_in_bf, w_ff_in_bf, w_ff_out_bf]
    if has_final:
        in_specs.append(pl.BlockSpec((1, D_MODEL), lambda b, i, j: (0, 0)))
        args.append(norm_final)
    return pl.pallas_call(
        functools.partial(_ffn_kernel, has_final=has_final),
        grid=(bsz, n // tm, nj),
        in_specs=in_specs,
        out_specs=pl.BlockSpec((None, tm, D_MODEL), lambda b, i, j: (b, i, 0)),
        out_shape=jax.ShapeDtypeStruct((bsz, n, D_MODEL), F32),
        scratch_shapes=[pltpu.VMEM((tm, D_MODEL), BF16)],
        compiler_params=_params(("parallel", "parallel", "arbitrary")),
        name="ffn",
    )(*args)


def grid_pos_embed(n):
    rows = n // GRID_W
    r, col = jnp.meshgrid(jnp.arange(rows, dtype=F32), jnp.arange(GRID_W, dtype=F32), indexing="ij")
    quarter = D_MODEL // 4
    omega = 1.0 / (POS_BASE ** (jnp.arange(quarter, dtype=F32) / quarter))

    def enc(pos):
        a = pos.reshape(-1)[:, None] * omega[None, :]
        return jnp.concatenate([jnp.sin(a), jnp.cos(a)], axis=-1)

    return jnp.concatenate([enc(r), enc(col)], axis=-1)


def kernel(x_prompt, x_sample, state_hgrn, c, c_ctx, w_mod, b_mod, norm_mix, norm_ffn, w_in, lb_raw,
           g_norm, w_a, w_b, w_out, w_ff_in, w_ff_out, norm_final):
    bc, nc, _ = x_prompt.shape
    bl, nl, _ = x_sample.shape
    assert 1 + bl <= MOD_ROWS

    cvec = jnp.zeros((MOD_ROWS, D_MODEL), F32).at[0].set(c_ctx).at[1:1 + bl].set(c)
    mod = adaln_table(cvec, w_mod, b_mod)

    w_in_bf, w_a_bf, w_b_bf, w_out_bf = (w.astype(BF16) for w in (w_in, w_a, w_b, w_out))
    w_ff_in_bf, w_ff_out_bf = w_ff_in.astype(BF16), w_ff_out.astype(BF16)
    norm_mix3 = norm_mix.reshape(DEPTH, 1, D_MODEL)
    norm_ffn3 = norm_ffn.reshape(DEPTH, 1, D_MODEL)
    g_norm3 = g_norm.reshape(DEPTH, 1, DV)
    norm_final2 = norm_final.reshape(1, D_MODEL)
    lb_dir = jnp.transpose(lb_raw, (1, 0, 2))

    fidx = jnp.arange(FG, dtype=jnp.int32)
    fang = ((fidx[:, None] * fidx[None, :]) % FG).astype(F32) * (2.0 * math.pi / FG)
    cs_f = jnp.concatenate([jnp.cos(fang), jnp.sin(fang)], axis=1).astype(BF16)
    tables = {n: dft_tables(n) for n in {nc, nl}}
    pe = grid_pos_embed(nl)

    def layer(x, seq_shape, pe_l, state, l, row0):
        bt, nt, _ = x.shape
        bs, ns = seq_shape
        p = proj_in(x, pe_l, mod, norm_mix3, w_in_bf, l, row0)
        ps = p.reshape(bs, ns, N_IN)
        o_f, s_f = scan(ps, lb_dir, state, l, 0)
        o_b, s_b = scan(ps, lb_dir, state, l, 1)
        fa, fb = chan_dft(ps, cs_f)
        z = seq_dft(fa, fb, *tables[ns])
        x = mix_out(o_f.reshape(bt, nt, W_A), o_b.reshape(bt, nt, W_A), p, z.reshape(bt, nt, W_B),
                    x, pe_l, mod, g_norm3, w_a_bf, w_b_bf, w_out_bf, l, row0)
        x = ffn(x, mod, norm_ffn3, w_ff_in_bf, w_ff_out_bf,
                norm_final2 if l == DEPTH - 1 else None, l, row0)
        return x, s_f, s_b

    xc = x_prompt.reshape(1, bc * nc, D_MODEL)
    xs = x_sample
    new_states = []
    for l in range(DEPTH):
        xc, s_f, s_b = layer(xc, (bc, nc), None, None, l, 0)
        new_states.append(jnp.stack([s_f, s_b], axis=1))
        xs, _, _ = layer(xs, (bl, nl), pe if l == 0 else None, state_hgrn, l, 1)
    state_new = jnp.stack(new_states, axis=1).astype(x_prompt.dtype)
    return xc.reshape(bc, nc, D_MODEL), xs, state_new
```

```python
import functools
import math

import jax
import jax.numpy as jnp
from jax import lax
from jax.experimental import pallas as pl
from jax.experimental.pallas import tpu as pltpu

D_MODEL = 2048
DEPTH = 2
H_A = 8
DK = 128
DV = 128
W_A = H_A * DK
N_FG = 4
FG = 256
W_B = N_FG * FG
D_FF = 5632
N_IN = 5 * W_A + W_B + 2 * D_MODEL
EPS = 1e-6
GRID_W = 64
POS_BASE = 10000.0

COL_Q = 0
COL_FF = W_A
COL_FB = 2 * W_A
COL_IV = 3 * W_A
COL_OG = 4 * W_A
COL_U = 5 * W_A
COL_GA = 5 * W_A + W_B
COL_GB = 5 * W_A + W_B + D_MODEL

MOD_ROWS = 8
LANES = 128
SUBLANES = 8
VMEM_LIMIT = 56 * 1024 * 1024

TM_IN, TN_IN, TN_IN_PE = 1024, 1024, 512
TM_MIX = 256
TM_FFN, TN_FFN = 1024, 256
TM_CDFT = 512
TK_SDFT = 256
SCAN_ROWS = 256
SCAN_CHUNK = 128
SCAN_LEVELS = (64, 32, 16, 8)
SCAN_HEADS = 4
LOG2E = 1.4426950408889634

F32 = jnp.float32
BF16 = jnp.bfloat16


def _params(sem):
    return pltpu.CompilerParams(dimension_semantics=sem, vmem_limit_bytes=VMEM_LIMIT)


def _silu(x):
    return x * jax.nn.sigmoid(x)


def _rms(x):
    return x * lax.rsqrt(jnp.mean(x * x, axis=-1, keepdims=True) + EPS)


def _adaln_kernel(c_ref, w_ref, b_ref, out_ref):
    s = _silu(c_ref[...]).astype(BF16)
    out_ref[...] = jnp.dot(s, w_ref[...].astype(BF16), preferred_element_type=F32) + b_ref[...]


def adaln_table(cvec, w_mod, b_mod):
    tn = 1024
    out = pl.pallas_call(
        _adaln_kernel,
        grid=(DEPTH, 6 * D_MODEL // tn),
        in_specs=[
            pl.BlockSpec((MOD_ROWS, D_MODEL), lambda l, j: (0, 0)),
            pl.BlockSpec((None, D_MODEL, tn), lambda l, j: (l, 0, j)),
            pl.BlockSpec((None, 1, tn), lambda l, j: (l, 0, j)),
        ],
        out_specs=pl.BlockSpec((None, MOD_ROWS, tn), lambda l, j: (l, 0, j)),
        out_shape=jax.ShapeDtypeStruct((DEPTH, MOD_ROWS, 6 * D_MODEL), F32),
        compiler_params=_params(("parallel", "parallel")),
        name="adaln",
    )(cvec, w_mod, b_mod.reshape(DEPTH, 1, 6 * D_MODEL))
    return out.reshape(DEPTH, MOD_ROWS, 6, D_MODEL)


def _proj_in_kernel(*refs, has_pe):
    if has_pe:
        x_ref, pe_ref, mod_ref, n_ref, w_ref, out_ref, h_ref = refs
    else:
        x_ref, mod_ref, n_ref, w_ref, out_ref, h_ref = refs

    @pl.when(pl.program_id(2) == 0)
    def _():
        x = x_ref[...]
        if has_pe:
            x = x + pe_ref[...]
        h = _rms(x) * n_ref[...]
        h = h * (1.0 + mod_ref[1:2, :]) + mod_ref[0:1, :]
        h_ref[...] = h.astype(BF16)

    out_ref[...] = jnp.dot(h_ref[...], w_ref[...], preferred_element_type=F32)


def proj_in(x, pe, mod, norm_w, w_in_bf, layer, row0):
    bsz, n, _ = x.shape
    tm = min(TM_IN, n)
    has_pe = pe is not None
    tn = TN_IN_PE if has_pe else TN_IN
    in_specs = [pl.BlockSpec((None, tm, D_MODEL), lambda b, i, j: (b, i, 0))]
    args = [x]
    if has_pe:
        in_specs.append(pl.BlockSpec((tm, D_MODEL), lambda b, i, j: (i, 0)))
        args.append(pe)
    in_specs += [
        pl.BlockSpec((None, None, 6, D_MODEL), lambda b, i, j: (layer, row0 + b, 0, 0)),
        pl.BlockSpec((None, 1, D_MODEL), lambda b, i, j: (layer, 0, 0)),
        pl.BlockSpec((None, D_MODEL, tn), lambda b, i, j: (layer, 0, j)),
    ]
    args += [mod, norm_w, w_in_bf]
    return pl.pallas_call(
        functools.partial(_proj_in_kernel, has_pe=has_pe),
        grid=(bsz, n // tm, N_IN // tn),
        in_specs=in_specs,
        out_specs=pl.BlockSpec((None, tm, tn), lambda b, i, j: (b, i, j)),
        out_shape=jax.ShapeDtypeStruct((bsz, n, N_IN), F32),
        scratch_shapes=[pltpu.VMEM((tm, D_MODEL), BF16)],
        compiler_params=_params(("parallel", "parallel", "arbitrary")),
        name="proj_in",
    )(*args)


def _nt(a, b):
    return lax.dot_general(a, b, (((1,), (1,)), ((), ())), preferred_element_type=F32)


def _tn(a, b):
    return lax.dot_general(a, b, (((0,), (0,)), ((), ())), preferred_element_type=F32)


def _chunk(q_ref, k_ref, b_ref, v_ref, base, sl, st, reverse, level_masks, row_masks):
    def rows(ref, r0, n):
        return ref[pl.ds(base + r0, n), sl]

    def row(ref, r):
        return ref[pl.ds(base + r, 1), sl]

    q = rows(q_ref, 0, SCAN_CHUNK)
    k = rows(k_ref, 0, SCAN_CHUNK)
    b = rows(b_ref, 0, SCAN_CHUNK)
    v = rows(v_ref, 0, SCAN_CHUNK)
    vb = v.astype(BF16)
    b_end = row(b_ref, 0 if reverse else SCAN_CHUNK - 1)

    o = _nt((q * jnp.exp2(b)).astype(BF16), st.astype(BF16))

    a = jnp.zeros((SCAN_CHUNK, SCAN_CHUNK), F32)
    for h, m in zip(SCAN_LEVELS, level_masks):
        z = jnp.zeros((h, LANES), F32)
        qp, kp = [], []
        for r0 in range(0, SCAN_CHUNK, 2 * h):
            if reverse:
                beta = row(b_ref, r0 + h)
                qp += [rows(q_ref, r0, h) * jnp.exp2(rows(b_ref, r0, h) - beta), z]
                kp += [z, rows(k_ref, r0 + h, h) * jnp.exp2(beta - rows(b_ref, r0 + h, h))]
            else:
                beta = row(b_ref, r0 + h - 1)
                kp += [rows(k_ref, r0, h) * jnp.exp2(beta - rows(b_ref, r0, h)), z]
                qp += [z, rows(q_ref, r0 + h, h) * jnp.exp2(rows(b_ref, r0 + h, h) - beta)]
        p = _nt(jnp.concatenate(qp, axis=0).astype(BF16), jnp.concatenate(kp, axis=0).astype(BF16))
        a = jnp.where(m, p, a)
    o = o + jnp.dot(a.astype(BF16), vb, preferred_element_type=F32)

    parts = []
    for r0 in range(0, SCAN_CHUNK, SUBLANES):
        qv = rows(q_ref, r0, SUBLANES)
        bv = rows(b_ref, r0, SUBLANES)
        acc = jnp.zeros((SUBLANES, LANES), F32)
        for j in range(SUBLANES):
            shape = (SUBLANES, LANES)
            kj = jnp.broadcast_to(row(k_ref, r0 + j), shape)
            bj = jnp.broadcast_to(row(b_ref, r0 + j), shape)
            vj = jnp.broadcast_to(row(v_ref, r0 + j), shape)
            d = bv - bj
            if row_masks[j] is not None:
                d = jnp.where(row_masks[j], d, -1e30)
            pr = qv * kj * jnp.exp2(d)
            acc = acc + jnp.sum(pr, axis=1, keepdims=True) * vj
        parts.append(acc)
    o = o + jnp.concatenate(parts, axis=0)

    khat = (k * jnp.exp2(b_end - b)).astype(BF16)
    st = st * jnp.exp2(b_end) + _tn(vb, khat)
    return o, st


def _scan_kernel(*refs, layer, reverse, has_s0):
    if has_s0:
        q_ref, x_ref, v_ref, lb_ref, s0_ref, o_ref, sfin_ref, st_ref, k_scr, b_scr = refs
    else:
        q_ref, x_ref, v_ref, lb_ref, o_ref, sfin_ref, st_ref, k_scr, b_scr = refs
    i = pl.program_id(2)
    nch = q_ref.shape[0] // SCAN_CHUNK
    width = q_ref.shape[1]

    @pl.when(i == 0)
    def _():
        for hh in range(SCAN_HEADS):
            if has_s0:
                st_ref[hh] = s0_ref[hh].T
            else:
                st_ref[hh] = jnp.zeros((DV, DK), F32)

    t_idx = lax.broadcasted_iota(jnp.int32, (SCAN_CHUNK, SCAN_CHUNK), 0)
    s_idx = lax.broadcasted_iota(jnp.int32, (SCAN_CHUNK, SCAN_CHUNK), 1)
    incl = (s_idx >= t_idx) if reverse else (s_idx <= t_idx)
    strict = (s_idx > t_idx) if reverse else (s_idx < t_idx)
    tmat = jnp.where(incl, 1.0, 0.0).astype(BF16)
    diff = t_idx ^ s_idx
    level_masks = [(diff >= h) & (diff < 2 * h) & strict for h in SCAN_LEVELS]
    sub = lax.broadcasted_iota(jnp.int32, (SUBLANES, LANES), 0)
    if reverse:
        row_masks = [(sub <= j) if j < SUBLANES - 1 else None for j in range(SUBLANES)]
    else:
        row_masks = [(sub >= j) if j > 0 else None for j in range(SUBLANES)]

    lbr = lb_ref[...]
    e = jnp.exp(lbr - jnp.max(lbr, axis=0, keepdims=True))
    lb_all = jnp.sum(e[1:layer + 1], axis=0, keepdims=True) / jnp.sum(e, axis=0, keepdims=True) \
        if layer > 0 else jnp.zeros((1, width), F32)

    for c in (range(nch - 1, -1, -1) if reverse else range(nch)):
        base = SCAN_CHUNK * c
        rs = pl.ds(base, SCAN_CHUNK)
        f = lb_all + (1.0 - lb_all) * jax.nn.sigmoid(x_ref[rs, :])
        g = jnp.log(f) * LOG2E
        ghi = g.astype(BF16)
        glo = (g - ghi.astype(F32)).astype(BF16)
        b2 = jnp.dot(tmat, jnp.concatenate([ghi, glo], axis=1), preferred_element_type=F32)
        k_scr[rs, :] = 1.0 - f
        b_scr[rs, :] = b2[:, :width] + b2[:, width:]
        for hh in range(SCAN_HEADS):
            sl = slice(LANES * hh, LANES * (hh + 1))
            o, st = _chunk(q_ref, k_scr, b_scr, v_ref, base, sl, st_ref[hh], reverse,
                           level_masks, row_masks)
            o_ref[rs, sl] = o
            st_ref[hh] = st

    @pl.when(i == pl.num_programs(2) - 1)
    def _():
        for hh in range(SCAN_HEADS):
            sfin_ref[hh] = st_ref[hh].T


def scan(p, lb_dir, state, layer, direction):
    bsz, n, _ = p.shape
    reverse = direction == 1
    nblk = n // SCAN_ROWS
    wblk = LANES * SCAN_HEADS
    has_s0 = state is not None

    def rowblk(i):
        return nblk - 1 - i if reverse else i

    def col(c0):
        return lambda b, h, i: (b, rowblk(i), c0 // wblk + h)

    in_specs = [
        pl.BlockSpec((None, SCAN_ROWS, wblk), col(COL_Q)),
        pl.BlockSpec((None, SCAN_ROWS, wblk), col(COL_FB if reverse else COL_FF)),
        pl.BlockSpec((None, SCAN_ROWS, wblk), col(COL_IV)),
        pl.BlockSpec((None, DEPTH, wblk), lambda b, h, i: (direction, 0, h)),
    ]
    args = [p, p, p, lb_dir]
    if has_s0:
        in_specs.append(pl.BlockSpec((None, None, None, SCAN_HEADS, DK, DV),
                                     lambda b, h, i: (b, layer, direction, h, 0, 0)))
        args.append(state)
    return pl.pallas_call(
        functools.partial(_scan_kernel, layer=layer, reverse=reverse, has_s0=has_s0),
        grid=(bsz, H_A // SCAN_HEADS, nblk),
        in_specs=in_specs,
        out_specs=[
            pl.BlockSpec((None, SCAN_ROWS, wblk), lambda b, h, i: (b, rowblk(i), h)),
            pl.BlockSpec((None, SCAN_HEADS, DK, DV), lambda b, h, i: (b, h, 0, 0)),
        ],
        out_shape=[
            jax.ShapeDtypeStruct((bsz, n, W_A), F32),
            jax.ShapeDtypeStruct((bsz, H_A, DK, DV), F32),
        ],
        scratch_shapes=[pltpu.VMEM((SCAN_HEADS, DV, DK), F32),
                        pltpu.VMEM((SCAN_ROWS, wblk), F32),
                        pltpu.VMEM((SCAN_ROWS, wblk), F32)],
        compiler_params=_params(("parallel", "parallel", "arbitrary")),
        name="scan_bwd" if reverse else "scan_fwd",
    )(*args)


def _chan_dft_kernel(u_ref, cs_ref, a_ref, b_ref):
    cs = cs_ref[...]
    for g in range(N_FG):
        sl = slice(FG * g, FG * (g + 1))
        r = jnp.dot(u_ref[:, sl].astype(BF16), cs, preferred_element_type=F32)
        a_ref[:, sl] = r[:, :FG].astype(BF16)
        b_ref[:, sl] = r[:, FG:].astype(BF16)


def chan_dft(p, cs_f):
    bsz, n, _ = p.shape
    tm = min(TM_CDFT, n)
    out = jax.ShapeDtypeStruct((bsz, n, W_B), BF16)
    spec = pl.BlockSpec((None, tm, W_B), lambda b, i: (b, i, 0))
    return pl.pallas_call(
        _chan_dft_kernel,
        grid=(bsz, n // tm),
        in_specs=[
            pl.BlockSpec((None, tm, W_B), lambda b, i: (b, i, COL_U // W_B)),
            pl.BlockSpec((FG, 2 * FG), lambda b, i: (0, 0)),
        ],
        out_specs=[spec, spec],
        out_shape=[out, out],
        compiler_params=_params(("parallel", "parallel")),
        name="chan_dft",
    )(p, cs_f)


def _seq_dft_kernel(c_ref, s_ref, a_ref, b_ref, z_ref, *, scale):
    acc = jnp.dot(c_ref[...], a_ref[...], preferred_element_type=F32)
    acc = acc - jnp.dot(s_ref[...], b_ref[...], preferred_element_type=F32)
    z_ref[...] = (acc * scale).astype(BF16)


def seq_dft(a, b, cos_n, sin_n):
    bsz, n, _ = a.shape
    tk = min(TK_SDFT, n)
    full = pl.BlockSpec((None, n, W_B), lambda bi, i: (bi, 0, 0))
    slab = pl.BlockSpec((tk, n), lambda bi, i: (i, 0))
    return pl.pallas_call(
        functools.partial(_seq_dft_kernel, scale=1.0 / math.sqrt(n * FG)),
        grid=(bsz, n // tk),
        in_specs=[slab, slab, full, full],
        out_specs=pl.BlockSpec((None, tk, W_B), lambda bi, i: (bi, i, 0)),
        out_shape=jax.ShapeDtypeStruct((bsz, n, W_B), BF16),
        compiler_params=_params(("parallel", "parallel")),
        name="seq_dft",
    )(cos_n, sin_n, a, b)


def dft_tables(n):
    idx = jnp.arange(n, dtype=jnp.int32)
    ang = ((idx[:, None] * idx[None, :]) % n).astype(F32) * (2.0 * math.pi / n)
    return jnp.cos(ang).astype(BF16), jnp.sin(ang).astype(BF16)


DFT_RADIX = 64
DFT_GROUP = 8
DFT_STEP = 16


def _kron_rows(mat, inner):
    r, c = mat.shape
    eye = jnp.eye(inner, dtype=mat.dtype)
    return (mat[:, None, :, None] * eye[None, :, None, :]).reshape(r * inner, c * inner)


def dft2_constants(n):
    m = DFT_RADIX
    idx = jnp.arange(m, dtype=jnp.int32)
    ang = ((idx[:, None] * idx[None, :]) % m).astype(F32) * (2.0 * math.pi / m)
    cm, sm = jnp.cos(ang), jnp.sin(ang)
    ck, sk = _kron_rows(cm, DFT_GROUP), _kron_rows(sm, DFT_GROUP)
    l1 = jnp.concatenate([jnp.concatenate([ck, -sk], axis=1),
                          jnp.concatenate([sk, ck], axis=1)], axis=0).astype(BF16)
    eye = jnp.eye(DFT_GROUP, dtype=F32)
    c2 = (cm[:, None, None, :] * eye[None, :, :, None]).reshape(m * DFT_GROUP, DFT_GROUP * m)
    s2 = (sm[:, None, None, :] * eye[None, :, :, None]).reshape(m * DFT_GROUP, DFT_GROUP * m)
    l2 = jnp.concatenate([c2, -s2], axis=1).astype(BF16)
    k1 = jnp.arange(m, dtype=jnp.int32)[None, :, None]
    n2 = (jnp.arange(m // DFT_GROUP, dtype=jnp.int32)[:, None, None] * DFT_GROUP
          + jnp.arange(DFT_GROUP, dtype=jnp.int32)[None, None, :])
    tang = ((k1 * n2) % n).astype(F32) * (2.0 * math.pi / n)
    tang = jnp.broadcast_to(tang.reshape(m // DFT_GROUP, m * DFT_GROUP, 1),
                            (m // DFT_GROUP, m * DFT_GROUP, LANES))
    return l1, l2, jnp.cos(tang), jnp.sin(tang)


def _dft_stage1_kernel(u_ref, cs_ref, l1_ref, tc_ref, ts_ref, yr_ref, yi_ref):
    m = DFT_RADIX
    rows = m * DFT_GROUP
    cs = cs_ref[...]
    l1 = l1_ref[...]
    yr_parts, yi_parts = [], []
    for hf in range(DFT_STEP // DFT_GROUP):
        u = u_ref[:, DFT_GROUP * hf:DFT_GROUP * (hf + 1), :].reshape(rows, W_B)
        a_parts, b_parts = [], []
        for g in range(N_FG):
            r = jnp.dot(u[:, FG * g:FG * (g + 1)].astype(BF16), cs, preferred_element_type=F32)
            a_parts.append(r[:, :FG].astype(BF16))
            b_parts.append(r[:, FG:].astype(BF16))
        x = jnp.concatenate([jnp.concatenate(a_parts, axis=1), jnp.concatenate(b_parts, axis=1)], axis=0)
        y = jnp.dot(l1, x, preferred_element_type=F32)
        yr, yi = y[:rows], y[rows:]
        tc, ts = tc_ref[hf], ts_ref[hf]
        pr, pi = [], []
        for c in range(W_B // LANES):
            sl = slice(LANES * c, LANES * (c + 1))
            pr.append(yr[:, sl] * tc - yi[:, sl] * ts)
            pi.append(yr[:, sl] * ts + yi[:, sl] * tc)
        yr_parts.append(jnp.concatenate(pr, axis=1).reshape(m, DFT_GROUP, W_B))
        yi_parts.append(jnp.concatenate(pi, axis=1).reshape(m, DFT_GROUP, W_B))
    yr_ref[...] = jnp.concatenate(yr_parts, axis=1).astype(BF16)
    yi_ref[...] = jnp.concatenate(yi_parts, axis=1).astype(BF16)


def _dft_stage2_kernel(yr_ref, yi_ref, l2_ref, z_ref, *, scale):
    m = DFT_RADIX
    rows = m * DFT_GROUP
    l2 = l2_ref[...]
    parts = []
    for hf in range(DFT_STEP // DFT_GROUP):
        sl = slice(DFT_GROUP * hf, DFT_GROUP * (hf + 1))
        x = jnp.concatenate([yr_ref[sl].reshape(rows, W_B), yi_ref[sl].reshape(rows, W_B)], axis=0)
        z = jnp.dot(l2, x, preferred_element_type=F32) * scale
        parts.append(z.reshape(m, DFT_GROUP, W_B))
    z_ref[...] = jnp.concatenate(parts, axis=1).astype(BF16)


def fourier_two_stage(p, cs_f, consts):
    bsz, n, _ = p.shape
    m = DFT_RADIX
    l1, l2, tw_c, tw_s = consts
    p4 = p.reshape(bsz, m, m, N_IN)
    nstep = m // DFT_STEP
    per_step = DFT_STEP // DFT_GROUP
    ybuf = jax.ShapeDtypeStruct((bsz, m, m, W_B), BF16)
    yspec = pl.BlockSpec((None, m, DFT_STEP, W_B), lambda b, j: (b, 0, j, 0))
    twspec = pl.BlockSpec((per_step, m * DFT_GROUP, LANES), lambda b, j: (j, 0, 0))
    yr, yi = pl.pallas_call(
        _dft_stage1_kernel,
        grid=(bsz, nstep),
        in_specs=[
            pl.BlockSpec((None, m, DFT_STEP, W_B), lambda b, j: (b, 0, j, COL_U // W_B)),
            pl.BlockSpec((FG, 2 * FG), lambda b, j: (0, 0)),
            pl.BlockSpec(l1.shape, lambda b, j: (0, 0)),
            twspec, twspec,
        ],
        out_specs=[yspec, yspec],
        out_shape=[ybuf, ybuf],
        compiler_params=_params(("parallel", "parallel")),
        name="dft_stage1",
    )(p4, cs_f, l1, tw_c, tw_s)
    xspec = pl.BlockSpec((None, DFT_STEP, m, W_B), lambda b, i: (b, i, 0, 0))
    z = pl.pallas_call(
        functools.partial(_dft_stage2_kernel, scale=1.0 / math.sqrt(n * FG)),
        grid=(bsz, nstep),
        in_specs=[xspec, xspec, pl.BlockSpec(l2.shape, lambda b, i: (0, 0))],
        out_specs=pl.BlockSpec((None, m, DFT_STEP, W_B), lambda b, i: (b, 0, i, 0)),
        out_shape=jax.ShapeDtypeStruct((bsz, m, m, W_B), BF16),
        compiler_params=_params(("parallel", "parallel")),
        name="dft_stage2",
    )(yr, yi, l2)
    return z.reshape(bsz, n, W_B)


def _mix_out_kernel(*refs, has_pe):
    if has_pe:
        (of_ref, ob_ref, og_ref, z_ref, ga_ref, gb_ref, x_ref, pe_ref, mod_ref, gn_ref,
         wa_ref, wb_ref, wo_ref, out_ref) = refs
    else:
        (of_ref, ob_ref, og_ref, z_ref, ga_ref, gb_ref, x_ref, mod_ref, gn_ref,
         wa_ref, wb_ref, wo_ref, out_ref) = refs
    gn = gn_ref[...]
    parts = []
    for h in range(H_A):
        sl = slice(DV * h, DV * (h + 1))
        oh = _rms(of_ref[:, sl] + ob_ref[:, sl]) * gn
        parts.append((oh * _silu(og_ref[:, sl])).astype(BF16))
    o = jnp.concatenate(parts, axis=1)
    ya = jnp.dot(o, wa_ref[...], preferred_element_type=F32)
    yb = jnp.dot(z_ref[...], wb_ref[...], preferred_element_type=F32)
    merged = jax.nn.sigmoid(ga_ref[...]) * ya + jax.nn.sigmoid(gb_ref[...]) * yb
    m = jnp.dot(merged.astype(BF16), wo_ref[...], preferred_element_type=F32)
    x = x_ref[...]
    if has_pe:
        x = x + pe_ref[...]
    out_ref[...] = x + mod_ref[2:3, :] * m


def mix_out(o_f, o_b, p, z, x, pe, mod, g_norm, w_a_bf, w_b_bf, w_out_bf, layer, row0):
    bsz, n, _ = x.shape
    tm = min(TM_MIX, n)
    has_pe = pe is not None

    def tok(width, cblk=0):
        return pl.BlockSpec((None, tm, width), lambda b, i: (b, i, cblk))

    def whole(rows, cols):
        return pl.BlockSpec((None, rows, cols), lambda b, i: (layer, 0, 0),
                            pipeline_mode=pl.Buffered(1))

    in_specs = [tok(W_A), tok(W_A), tok(W_A, COL_OG // W_A), tok(W_B),
                tok(D_MODEL, COL_GA // D_MODEL), tok(D_MODEL, COL_GB // D_MODEL), tok(D_MODEL)]
    args = [o_f, o_b, p, z, p, p, x]
    if has_pe:
        in_specs.append(pl.BlockSpec((tm, D_MODEL), lambda b, i: (i, 0)))
        args.append(pe)
    in_specs += [
        pl.BlockSpec((None, None, 6, D_MODEL), lambda b, i: (layer, row0 + b, 0, 0)),
        pl.BlockSpec((None, 1, DV), lambda b, i: (layer, 0, 0)),
        whole(W_A, D_MODEL), whole(W_B, D_MODEL), whole(D_MODEL, D_MODEL),
    ]
    args += [mod, g_norm, w_a_bf, w_b_bf, w_out_bf]
    return pl.pallas_call(
        functools.partial(_mix_out_kernel, has_pe=has_pe),
        grid=(bsz, n // tm),
        in_specs=in_specs,
        out_specs=tok(D_MODEL),
        out_shape=jax.ShapeDtypeStruct((bsz, n, D_MODEL), F32),
        compiler_params=_params(("parallel", "parallel")),
        name="mix_out",
    )(*args)


def _ffn_kernel(*refs, has_final):
    if has_final:
        x_ref, mod_ref, n_ref, w1a_ref, w1g_ref, w2_ref, nf_ref, out_ref, h_ref = refs
    else:
        x_ref, mod_ref, n_ref, w1a_ref, w1g_ref, w2_ref, out_ref, h_ref = refs
    j = pl.program_id(2)

    @pl.when(j == 0)
    def _():
        h = _rms(x_ref[...]) * n_ref[...]
        h = h * (1.0 + mod_ref[4:5, :]) + mod_ref[3:4, :]
        h_ref[...] = h.astype(BF16)
        out_ref[...] = jnp.zeros_like(out_ref)

    h = h_ref[...]
    a = jnp.dot(h, w1a_ref[...], preferred_element_type=F32)
    gt = jnp.dot(h, w1g_ref[...], preferred_element_type=F32)
    act = (_silu(a) * gt).astype(BF16)
    out_ref[...] += jnp.dot(act, w2_ref[...], preferred_element_type=F32)

    @pl.when(j == pl.num_programs(2) - 1)
    def _():
        y = x_ref[...] + mod_ref[5:6, :] * out_ref[...]
        if has_final:
            y = _rms(y) * nf_ref[...]
        out_ref[...] = y


def ffn(x, mod, norm_w, w_ff_in_bf, w_ff_out_bf, norm_final, layer, row0):
    bsz, n, _ = x.shape
    tm = min(TM_FFN, n)
    nj = D_FF // TN_FFN
    has_final = norm_final is not None
    in_specs = [
        pl.BlockSpec((None, tm, D_MODEL), lambda b, i, j: (b, i, 0)),
        pl.BlockSpec((None, None, 6, D_MODEL), lambda b, i, j: (layer, row0 + b, 0, 0)),
        pl.BlockSpec((None, 1, D_MODEL), lambda b, i, j: (layer, 0, 0)),
        pl.BlockSpec((None, D_MODEL, TN_FFN), lambda b, i, j: (layer, 0, j)),
        pl.BlockSpec((None, D_MODEL, TN_FFN), lambda b, i, j: (layer, 0, nj + j)),
        pl.BlockSpec((None, TN_FFN, D_MODEL), lambda b, i, j: (layer, j, 0)),
    ]
    args = [x, mod, norm_w, w_ff_in_bf, w_ff_in_bf, w_ff_out_bf]
    if has_final:
        in_specs.append(pl.BlockSpec((1, D_MODEL), lambda b, i, j: (0, 0)))
        args.append(norm_final)
    return pl.pallas_call(
        functools.partial(_ffn_kernel, has_final=has_final),
        grid=(bsz, n // tm, nj),
        in_specs=in_specs,
        out_specs=pl.BlockSpec((None, tm, D_MODEL), lambda b, i, j: (b, i, 0)),
        out_shape=jax.ShapeDtypeStruct((bsz, n, D_MODEL), F32),
        scratch_shapes=[pltpu.VMEM((tm, D_MODEL), BF16)],
        compiler_params=_params(("parallel", "parallel", "arbitrary")),
        name="ffn",
    )(*args)


def grid_pos_embed(n):
    rows = n // GRID_W
    r, col = jnp.meshgrid(jnp.arange(rows, dtype=F32), jnp.arange(GRID_W, dtype=F32), indexing="ij")
    quarter = D_MODEL // 4
    omega = 1.0 / (POS_BASE ** (jnp.arange(quarter, dtype=F32) / quarter))

    def enc(pos):
        a = pos.reshape(-1)[:, None] * omega[None, :]
        return jnp.concatenate([jnp.sin(a), jnp.cos(a)], axis=-1)

    return jnp.concatenate([enc(r), enc(col)], axis=-1)


def kernel(x_prompt, x_sample, state_hgrn, c, c_ctx, w_mod, b_mod, norm_mix, norm_ffn, w_in, lb_raw,
           g_norm, w_a, w_b, w_out, w_ff_in, w_ff_out, norm_final):
    bc, nc, _ = x_prompt.shape
    bl, nl, _ = x_sample.shape
    assert 1 + bl <= MOD_ROWS

    cvec = jnp.zeros((MOD_ROWS, D_MODEL), F32).at[0].set(c_ctx).at[1:1 + bl].set(c)
    mod = adaln_table(cvec, w_mod, b_mod)

    w_in_bf, w_a_bf, w_b_bf, w_out_bf = (w.astype(BF16) for w in (w_in, w_a, w_b, w_out))
    w_ff_in_bf, w_ff_out_bf = w_ff_in.astype(BF16), w_ff_out.astype(BF16)
    norm_mix3 = norm_mix.reshape(DEPTH, 1, D_MODEL)
    norm_ffn3 = norm_ffn.reshape(DEPTH, 1, D_MODEL)
    g_norm3 = g_norm.reshape(DEPTH, 1, DV)
    norm_final2 = norm_final.reshape(1, D_MODEL)
    lb_dir = jnp.transpose(lb_raw, (1, 0, 2))

    fidx = jnp.arange(FG, dtype=jnp.int32)
    fang = ((fidx[:, None] * fidx[None, :]) % FG).astype(F32) * (2.0 * math.pi / FG)
    cs_f = jnp.concatenate([jnp.cos(fang), jnp.sin(fang)], axis=1).astype(BF16)
    tables = {n: dft2_constants(n) if n == DFT_RADIX ** 2 else dft_tables(n) for n in {nc, nl}}
    pe = grid_pos_embed(nl)

    def layer(x, seq_shape, pe_l, state, l, row0):
        bt, nt, _ = x.shape
        bs, ns = seq_shape
        p = proj_in(x, pe_l, mod, norm_mix3, w_in_bf, l, row0)
        ps = p.reshape(bs, ns, N_IN)
        o_f, s_f = scan(ps, lb_dir, state, l, 0)
        o_b, s_b = scan(ps, lb_dir, state, l, 1)
        if ns == DFT_RADIX ** 2:
            z = fourier_two_stage(ps, cs_f, tables[ns])
        else:
            fa, fb = chan_dft(ps, cs_f)
            z = seq_dft(fa, fb, *tables[ns])
        x = mix_out(o_f.reshape(bt, nt, W_A), o_b.reshape(bt, nt, W_A), p, z.reshape(bt, nt, W_B),
                    x, pe_l, mod, g_norm3, w_a_bf, w_b_bf, w_out_bf, l, row0)
        x = ffn(x, mod, norm_ffn3, w_ff_in_bf, w_ff_out_bf,
                norm_final2 if l == DEPTH - 1 else None, l, row0)
        return x, s_f, s_b

    xc = x_prompt.reshape(1, bc * nc, D_MODEL)
    xs = x_sample
    new_states = []
    for l in range(DEPTH):
        xc, s_f, s_b = layer(xc, (bc, nc), None, None, l, 0)
        new_states.append(jnp.stack([s_f, s_b], axis=1))
        xs, _, _ = layer(xs, (bl, nl), pe if l == 0 else None, state_hgrn, l, 1)
    state_new = jnp.stack(new_states, axis=1).astype(x_prompt.dtype)
    return xc.reshape(bc, nc, D_MODEL), xs, state_new
```

```python
import functools
import math

import jax
import jax.numpy as jnp
from jax import lax
from jax.experimental import pallas as pl
from jax.experimental.pallas import tpu as pltpu

D_MODEL = 2048
DEPTH = 2
H_A = 8
DK = 128
DV = 128
W_A = H_A * DK
N_FG = 4
FG = 256
W_B = N_FG * FG
D_FF = 5632
N_IN = 5 * W_A + W_B + 2 * D_MODEL
EPS = 1e-6
GRID_W = 64
POS_BASE = 10000.0

N_REC = 4 * W_A
N_GATE = N_IN - N_REC
COL_Q = 0
COL_FF = W_A
COL_FB = 2 * W_A
COL_IV = 3 * W_A
GCOL_OG = 0
GCOL_U = W_A
GCOL_GA = W_A + W_B
GCOL_GB = W_A + W_B + D_MODEL

MOD_ROWS = 8
LANES = 128
SUBLANES = 8
VMEM_LIMIT = 56 * 1024 * 1024

TM_IN, TN_IN = 1024, 1024
TM_MIX = 256
TM_FFN, TN_FFN = 1024, 256
TM_CDFT = 512
TK_SDFT = 256
SCAN_ROWS = 256
SCAN_CHUNK = 128
SCAN_LEVELS = (64, 32, 16, 8)
SCAN_HEADS = 4
LOG2E = 1.4426950408889634

F32 = jnp.float32
BF16 = jnp.bfloat16


def _params(sem):
    return pltpu.CompilerParams(dimension_semantics=sem, vmem_limit_bytes=VMEM_LIMIT)


def _silu(x):
    return x * jax.nn.sigmoid(x)


def _rms(x):
    return x * lax.rsqrt(jnp.mean(x * x, axis=-1, keepdims=True) + EPS)


def _adaln_kernel(c_ref, w_ref, b_ref, out_ref):
    s = _silu(c_ref[...]).astype(BF16)
    out_ref[...] = jnp.dot(s, w_ref[...].astype(BF16), preferred_element_type=F32) + b_ref[...]


def adaln_table(cvec, w_mod, b_mod):
    tn = 1024
    out = pl.pallas_call(
        _adaln_kernel,
        grid=(DEPTH, 6 * D_MODEL // tn),
        in_specs=[
            pl.BlockSpec((MOD_ROWS, D_MODEL), lambda l, j: (0, 0)),
            pl.BlockSpec((None, D_MODEL, tn), lambda l, j: (l, 0, j)),
            pl.BlockSpec((None, 1, tn), lambda l, j: (l, 0, j)),
        ],
        out_specs=pl.BlockSpec((None, MOD_ROWS, tn), lambda l, j: (l, 0, j)),
        out_shape=jax.ShapeDtypeStruct((DEPTH, MOD_ROWS, 6 * D_MODEL), F32),
        compiler_params=_params(("parallel", "parallel")),
        name="adaln",
    )(cvec, w_mod, b_mod.reshape(DEPTH, 1, 6 * D_MODEL))
    return out.reshape(DEPTH, MOD_ROWS, 6, D_MODEL)


def _proj_in_kernel(x_ref, mod_ref, n_ref, w_ref, rec_ref, gate_ref, h_ref):
    j = pl.program_id(2)

    @pl.when(j == 0)
    def _():
        h = _rms(x_ref[...]) * n_ref[...]
        h = h * (1.0 + mod_ref[1:2, :]) + mod_ref[0:1, :]
        h_ref[...] = h.astype(BF16)

    res = jnp.dot(h_ref[...], w_ref[...], preferred_element_type=F32)
    n_rec = N_REC // w_ref.shape[1]

    @pl.when(j < n_rec)
    def _():
        rec_ref[...] = res

    @pl.when(j >= n_rec)
    def _():
        gate_ref[...] = res.astype(BF16)


def proj_in(x, mod, norm_w, w_in_bf, layer, row0):
    bsz, n, _ = x.shape
    tm = min(TM_IN, n)
    tn = TN_IN
    n_rec = N_REC // tn
    return pl.pallas_call(
        _proj_in_kernel,
        grid=(bsz, n // tm, N_IN // tn),
        in_specs=[
            pl.BlockSpec((None, tm, D_MODEL), lambda b, i, j: (b, i, 0)),
            pl.BlockSpec((None, None, 6, D_MODEL), lambda b, i, j: (layer, row0 + b, 0, 0)),
            pl.BlockSpec((None, 1, D_MODEL), lambda b, i, j: (layer, 0, 0)),
            pl.BlockSpec((None, D_MODEL, tn), lambda b, i, j: (layer, 0, j)),
        ],
        out_specs=[
            pl.BlockSpec((None, tm, tn), lambda b, i, j: (b, i, jnp.minimum(j, n_rec - 1))),
            pl.BlockSpec((None, tm, tn), lambda b, i, j: (b, i, jnp.maximum(j - n_rec, 0))),
        ],
        out_shape=[jax.ShapeDtypeStruct((bsz, n, N_REC), F32),
                   jax.ShapeDtypeStruct((bsz, n, N_GATE), BF16)],
        scratch_shapes=[pltpu.VMEM((tm, D_MODEL), BF16)],
        compiler_params=_params(("parallel", "parallel", "arbitrary")),
        name="proj_in",
    )(x, mod, norm_w, w_in_bf)


def _nt(a, b):
    return lax.dot_general(a, b, (((1,), (1,)), ((), ())), preferred_element_type=F32)


def _tn(a, b):
    return lax.dot_general(a, b, (((0,), (0,)), ((), ())), preferred_element_type=F32)


def _chunk(q_ref, k_ref, b_ref, v_ref, base, sl, st, reverse, level_masks, row_masks):
    def rows(ref, r0, n):
        return ref[pl.ds(base + r0, n), sl]

    def row(ref, r):
        return ref[pl.ds(base + r, 1), sl]

    q = rows(q_ref, 0, SCAN_CHUNK)
    k = rows(k_ref, 0, SCAN_CHUNK)
    b = rows(b_ref, 0, SCAN_CHUNK)
    v = rows(v_ref, 0, SCAN_CHUNK)
    vb = v.astype(BF16)
    b_end = row(b_ref, 0 if reverse else SCAN_CHUNK - 1)

    o = _nt((q * jnp.exp2(b)).astype(BF16), st.astype(BF16))

    a = jnp.zeros((SCAN_CHUNK, SCAN_CHUNK), F32)
    for h, m in zip(SCAN_LEVELS, level_masks):
        z = jnp.zeros((h, LANES), F32)
        qp, kp = [], []
        for r0 in range(0, SCAN_CHUNK, 2 * h):
            if reverse:
                beta = row(b_ref, r0 + h)
                qp += [rows(q_ref, r0, h) * jnp.exp2(rows(b_ref, r0, h) - beta), z]
                kp += [z, rows(k_ref, r0 + h, h) * jnp.exp2(beta - rows(b_ref, r0 + h, h))]
            else:
                beta = row(b_ref, r0 + h - 1)
                kp += [rows(k_ref, r0, h) * jnp.exp2(beta - rows(b_ref, r0, h)), z]
                qp += [z, rows(q_ref, r0 + h, h) * jnp.exp2(rows(b_ref, r0 + h, h) - beta)]
        p = _nt(jnp.concatenate(qp, axis=0).astype(BF16), jnp.concatenate(kp, axis=0).astype(BF16))
        a = jnp.where(m, p, a)
    o = o + jnp.dot(a.astype(BF16), vb, preferred_element_type=F32)

    parts = []
    for r0 in range(0, SCAN_CHUNK, SUBLANES):
        qv = rows(q_ref, r0, SUBLANES)
        bv = rows(b_ref, r0, SUBLANES)
        acc = jnp.zeros((SUBLANES, LANES), F32)
        for j in range(SUBLANES):
            shape = (SUBLANES, LANES)
            kj = jnp.broadcast_to(row(k_ref, r0 + j), shape)
            bj = jnp.broadcast_to(row(b_ref, r0 + j), shape)
            vj = jnp.broadcast_to(row(v_ref, r0 + j), shape)
            d = bv - bj
            if row_masks[j] is not None:
                d = jnp.where(row_masks[j], d, -1e30)
            pr = qv * kj * jnp.exp2(d)
            acc = acc + jnp.sum(pr, axis=1, keepdims=True) * vj
        parts.append(acc)
    o = o + jnp.concatenate(parts, axis=0)

    khat = (k * jnp.exp2(b_end - b)).astype(BF16)
    st = st * jnp.exp2(b_end) + _tn(vb, khat)
    return o, st


def _scan_kernel(*refs, layer, reverse, has_s0, has_gate, has_sbuf):
    refs = list(refs)
    q_ref, x_ref, v_ref, lb_ref = refs[:4]
    del refs[:4]
    s0_ref = refs.pop(0) if has_s0 else None
    of_ref, og_ref, gn_ref = (refs.pop(0), refs.pop(0), refs.pop(0)) if has_gate else (None, None, None)
    if has_sbuf:
        refs.pop(0)
    o_ref, sfin_ref, st_ref, k_scr, b_scr = refs
    i = pl.program_id(2)
    nch = q_ref.shape[0] // SCAN_CHUNK
    width = q_ref.shape[1]

    @pl.when(i == 0)
    def _():
        for hh in range(SCAN_HEADS):
            if has_s0:
                st_ref[hh] = s0_ref[hh].T
            else:
                st_ref[hh] = jnp.zeros((DV, DK), F32)

    t_idx = lax.broadcasted_iota(jnp.int32, (SCAN_CHUNK, SCAN_CHUNK), 0)
    s_idx = lax.broadcasted_iota(jnp.int32, (SCAN_CHUNK, SCAN_CHUNK), 1)
    incl = (s_idx >= t_idx) if reverse else (s_idx <= t_idx)
    strict = (s_idx > t_idx) if reverse else (s_idx < t_idx)
    tmat = jnp.where(incl, 1.0, 0.0).astype(BF16)
    diff = t_idx ^ s_idx
    level_masks = [(diff >= h) & (diff < 2 * h) & strict for h in SCAN_LEVELS]
    sub = lax.broadcasted_iota(jnp.int32, (SUBLANES, LANES), 0)
    if reverse:
        row_masks = [(sub <= j) if j < SUBLANES - 1 else None for j in range(SUBLANES)]
    else:
        row_masks = [(sub >= j) if j > 0 else None for j in range(SUBLANES)]

    lbr = lb_ref[...]
    e = jnp.exp(lbr - jnp.max(lbr, axis=0, keepdims=True))
    lb_all = jnp.sum(e[1:layer + 1], axis=0, keepdims=True) / jnp.sum(e, axis=0, keepdims=True) \
        if layer > 0 else jnp.zeros((1, width), F32)

    for c in (range(nch - 1, -1, -1) if reverse else range(nch)):
        base = SCAN_CHUNK * c
        rs = pl.ds(base, SCAN_CHUNK)
        f = lb_all + (1.0 - lb_all) * jax.nn.sigmoid(x_ref[rs, :])
        g = jnp.log(f) * LOG2E
        ghi = g.astype(BF16)
        glo = (g - ghi.astype(F32)).astype(BF16)
        b2 = jnp.dot(tmat, jnp.concatenate([ghi, glo], axis=1), preferred_element_type=F32)
        k_scr[rs, :] = 1.0 - f
        b_scr[rs, :] = b2[:, :width] + b2[:, width:]
        for hh in range(SCAN_HEADS):
            sl = slice(LANES * hh, LANES * (hh + 1))
            o, st = _chunk(q_ref, k_scr, b_scr, v_ref, base, sl, st_ref[hh], reverse,
                           level_masks, row_masks)
            if has_gate:
                o = _rms(o + of_ref[rs, sl]) * gn_ref[...]
                o_ref[rs, sl] = (o * _silu(og_ref[rs, sl].astype(F32))).astype(BF16)
            else:
                o_ref[rs, sl] = o
            st_ref[hh] = st

    @pl.when(i == pl.num_programs(2) - 1)
    def _():
        for hh in range(SCAN_HEADS):
            sfin_ref[hh] = st_ref[hh].T


def scan(p, lb_dir, state, layer, direction, gate=None, state_buf=None):
    bsz, n, _ = p.shape
    reverse = direction == 1
    nblk = n // SCAN_ROWS
    wblk = LANES * SCAN_HEADS
    has_s0 = state is not None
    has_gate = gate is not None
    has_sbuf = state_buf is not None

    def rowblk(i):
        return nblk - 1 - i if reverse else i

    def col(c0):
        return lambda b, h, i: (b, rowblk(i), c0 // wblk + h)

    in_specs = [
        pl.BlockSpec((None, SCAN_ROWS, wblk), col(COL_Q)),
        pl.BlockSpec((None, SCAN_ROWS, wblk), col(COL_FB if reverse else COL_FF)),
        pl.BlockSpec((None, SCAN_ROWS, wblk), col(COL_IV)),
        pl.BlockSpec((None, DEPTH, wblk), lambda b, h, i: (direction, 0, h)),
    ]
    args = [p, p, p, lb_dir]
    if has_s0:
        in_specs.append(pl.BlockSpec((None, None, None, SCAN_HEADS, DK, DV),
                                     lambda b, h, i: (b, layer, direction, h, 0, 0)))
        args.append(state)
    if has_gate:
        o_other, p_gate, g_norm = gate
        in_specs += [
            pl.BlockSpec((None, SCAN_ROWS, wblk), lambda b, h, i: (b, rowblk(i), h)),
            pl.BlockSpec((None, SCAN_ROWS, wblk), lambda b, h, i: (b, rowblk(i), GCOL_OG // wblk + h)),
            pl.BlockSpec((None, 1, DV), lambda b, h, i: (layer, 0, 0)),
        ]
        args += [o_other, p_gate, g_norm]
    aliases = {}
    if has_sbuf:
        aliases = {len(args): 1}
        in_specs.append(pl.BlockSpec(memory_space=pl.ANY))
        args.append(state_buf)
        s_spec = pl.BlockSpec((None, None, None, SCAN_HEADS, DK, DV),
                              lambda b, h, i: (b, layer, direction, h, 0, 0))
        s_shape = jax.ShapeDtypeStruct(state_buf.shape, F32)
    else:
        s_spec = pl.BlockSpec((None, SCAN_HEADS, DK, DV), lambda b, h, i: (b, h, 0, 0))
        s_shape = jax.ShapeDtypeStruct((bsz, H_A, DK, DV), F32)
    return pl.pallas_call(
        functools.partial(_scan_kernel, layer=layer, reverse=reverse, has_s0=has_s0,
                          has_gate=has_gate, has_sbuf=has_sbuf),
        grid=(bsz, H_A // SCAN_HEADS, nblk),
        in_specs=in_specs,
        out_specs=[
            pl.BlockSpec((None, SCAN_ROWS, wblk), lambda b, h, i: (b, rowblk(i), h)),
            s_spec,
        ],
        out_shape=[
            jax.ShapeDtypeStruct((bsz, n, W_A), BF16 if has_gate else F32),
            s_shape,
        ],
        input_output_aliases=aliases,
        scratch_shapes=[pltpu.VMEM((SCAN_HEADS, DV, DK), F32),
                        pltpu.VMEM((SCAN_ROWS, wblk), F32),
                        pltpu.VMEM((SCAN_ROWS, wblk), F32)],
        compiler_params=_params(("parallel", "parallel", "arbitrary")),
        name="scan_bwd" if reverse else "scan_fwd",
    )(*args)


def _chan_dft_kernel(u_ref, cs_ref, a_ref, b_ref):
    cs = cs_ref[...]
    for g in range(N_FG):
        sl = slice(FG * g, FG * (g + 1))
        r = jnp.dot(u_ref[:, sl], cs, preferred_element_type=F32)
        a_ref[:, sl] = r[:, :FG].astype(BF16)
        b_ref[:, sl] = r[:, FG:].astype(BF16)


def chan_dft(p, cs_f):
    bsz, n, _ = p.shape
    tm = min(TM_CDFT, n)
    out = jax.ShapeDtypeStruct((bsz, n, W_B), BF16)
    spec = pl.BlockSpec((None, tm, W_B), lambda b, i: (b, i, 0))
    return pl.pallas_call(
        _chan_dft_kernel,
        grid=(bsz, n // tm),
        in_specs=[
            pl.BlockSpec((None, tm, W_B), lambda b, i: (b, i, GCOL_U // W_B)),
            pl.BlockSpec((FG, 2 * FG), lambda b, i: (0, 0)),
        ],
        out_specs=[spec, spec],
        out_shape=[out, out],
        compiler_params=_params(("parallel", "parallel")),
        name="chan_dft",
    )(p, cs_f)


def _seq_dft_kernel(c_ref, s_ref, a_ref, b_ref, z_ref, *, scale):
    acc = jnp.dot(c_ref[...], a_ref[...], preferred_element_type=F32)
    acc = acc - jnp.dot(s_ref[...], b_ref[...], preferred_element_type=F32)
    z_ref[...] = (acc * scale).astype(BF16)


def seq_dft(a, b, cos_n, sin_n):
    bsz, n, _ = a.shape
    tk = min(TK_SDFT, n)
    full = pl.BlockSpec((None, n, W_B), lambda bi, i: (bi, 0, 0))
    slab = pl.BlockSpec((tk, n), lambda bi, i: (i, 0))
    return pl.pallas_call(
        functools.partial(_seq_dft_kernel, scale=1.0 / math.sqrt(n * FG)),
        grid=(bsz, n // tk),
        in_specs=[slab, slab, full, full],
        out_specs=pl.BlockSpec((None, tk, W_B), lambda bi, i: (bi, i, 0)),
        out_shape=jax.ShapeDtypeStruct((bsz, n, W_B), BF16),
        compiler_params=_params(("parallel", "parallel")),
        name="seq_dft",
    )(cos_n, sin_n, a, b)


def dft_tables(n):
    idx = jnp.arange(n, dtype=jnp.int32)
    ang = ((idx[:, None] * idx[None, :]) % n).astype(F32) * (2.0 * math.pi / n)
    return jnp.cos(ang).astype(BF16), jnp.sin(ang).astype(BF16)


DFT_RADIX = 64
DFT_GROUP = 8
DFT_STEP = 16


def _kron_rows(mat, inner):
    r, c = mat.shape
    eye = jnp.eye(inner, dtype=mat.dtype)
    return (mat[:, None, :, None] * eye[None, :, None, :]).reshape(r * inner, c * inner)


def dft2_constants(n):
    m = DFT_RADIX
    idx = jnp.arange(m, dtype=jnp.int32)
    ang = ((idx[:, None] * idx[None, :]) % m).astype(F32) * (2.0 * math.pi / m)
    cm, sm = jnp.cos(ang), jnp.sin(ang)
    ck, sk = _kron_rows(cm, DFT_GROUP), _kron_rows(sm, DFT_GROUP)
    l1 = jnp.concatenate([jnp.concatenate([ck, -sk], axis=1),
                          jnp.concatenate([sk, ck], axis=1)], axis=0).astype(BF16)
    eye = jnp.eye(DFT_GROUP, dtype=F32)
    c2 = (cm[:, None, None, :] * eye[None, :, :, None]).reshape(m * DFT_GROUP, DFT_GROUP * m)
    s2 = (sm[:, None, None, :] * eye[None, :, :, None]).reshape(m * DFT_GROUP, DFT_GROUP * m)
    l2 = jnp.concatenate([c2, -s2], axis=1).astype(BF16)
    k1 = jnp.arange(m, dtype=jnp.int32)[None, :, None]
    n2 = (jnp.arange(m // DFT_GROUP, dtype=jnp.int32)[:, None, None] * DFT_GROUP
          + jnp.arange(DFT_GROUP, dtype=jnp.int32)[None, None, :])
    tang = ((k1 * n2) % n).astype(F32) * (2.0 * math.pi / n)
    tang = jnp.broadcast_to(tang.reshape(m // DFT_GROUP, m * DFT_GROUP, 1),
                            (m // DFT_GROUP, m * DFT_GROUP, LANES))
    return l1, l2, jnp.cos(tang), jnp.sin(tang)


def _dft_stage1_kernel(u_ref, cs_ref, l1_ref, tc_ref, ts_ref, yr_ref, yi_ref):
    m = DFT_RADIX
    rows = m * DFT_GROUP
    cs = cs_ref[...]
    l1 = l1_ref[...]
    yr_parts, yi_parts = [], []
    u_all = u_ref[...].astype(F32)
    for hf in range(DFT_STEP // DFT_GROUP):
        u = u_all[:, DFT_GROUP * hf:DFT_GROUP * (hf + 1), :].reshape(rows, W_B)
        a_parts, b_parts = [], []
        for g in range(N_FG):
            r = jnp.dot(u[:, FG * g:FG * (g + 1)].astype(BF16), cs, preferred_element_type=F32)
            a_parts.append(r[:, :FG].astype(BF16))
            b_parts.append(r[:, FG:].astype(BF16))
        x = jnp.concatenate([jnp.concatenate(a_parts, axis=1), jnp.concatenate(b_parts, axis=1)], axis=0)
        y = jnp.dot(l1, x, preferred_element_type=F32)
        yr, yi = y[:rows], y[rows:]
        tc, ts = tc_ref[hf], ts_ref[hf]
        pr, pi = [], []
        for c in range(W_B // LANES):
            sl = slice(LANES * c, LANES * (c + 1))
            pr.append(yr[:, sl] * tc - yi[:, sl] * ts)
            pi.append(yr[:, sl] * ts + yi[:, sl] * tc)
        yr_parts.append(jnp.concatenate(pr, axis=1).reshape(m, DFT_GROUP, W_B))
        yi_parts.append(jnp.concatenate(pi, axis=1).reshape(m, DFT_GROUP, W_B))
    yr_ref[...] = jnp.concatenate(yr_parts, axis=1).astype(BF16)
    yi_ref[...] = jnp.concatenate(yi_parts, axis=1).astype(BF16)


def _dft_stage2_kernel(yr_ref, yi_ref, l2_ref, z_ref, *, scale):
    m = DFT_RADIX
    rows = m * DFT_GROUP
    l2 = l2_ref[...]
    parts = []
    for hf in range(DFT_STEP // DFT_GROUP):
        sl = slice(DFT_GROUP * hf, DFT_GROUP * (hf + 1))
        x = jnp.concatenate([yr_ref[sl].reshape(rows, W_B), yi_ref[sl].reshape(rows, W_B)], axis=0)
        z = jnp.dot(l2, x, preferred_element_type=F32) * scale
        parts.append(z.reshape(m, DFT_GROUP, W_B))
    z_ref[...] = jnp.concatenate(parts, axis=1).astype(BF16)


def fourier_two_stage(p, cs_f, consts):
    bsz, n, _ = p.shape
    m = DFT_RADIX
    l1, l2, tw_c, tw_s = consts
    p4 = p.reshape(bsz, m, m, N_GATE)
    nstep = m // DFT_STEP
    per_step = DFT_STEP // DFT_GROUP
    ybuf = jax.ShapeDtypeStruct((bsz, m, m, W_B), BF16)
    yspec = pl.BlockSpec((None, m, DFT_STEP, W_B), lambda b, j: (b, 0, j, 0))
    twspec = pl.BlockSpec((per_step, m * DFT_GROUP, LANES), lambda b, j: (j, 0, 0))
    yr, yi = pl.pallas_call(
        _dft_stage1_kernel,
        grid=(bsz, nstep),
        in_specs=[
            pl.BlockSpec((None, m, DFT_STEP, W_B), lambda b, j: (b, 0, j, GCOL_U // W_B)),
            pl.BlockSpec((FG, 2 * FG), lambda b, j: (0, 0)),
            pl.BlockSpec(l1.shape, lambda b, j: (0, 0)),
            twspec, twspec,
        ],
        out_specs=[yspec, yspec],
        out_shape=[ybuf, ybuf],
        compiler_params=_params(("parallel", "parallel")),
        name="dft_stage1",
    )(p4, cs_f, l1, tw_c, tw_s)
    xspec = pl.BlockSpec((None, DFT_STEP, m, W_B), lambda b, i: (b, i, 0, 0))
    z = pl.pallas_call(
        functools.partial(_dft_stage2_kernel, scale=1.0 / math.sqrt(n * FG)),
        grid=(bsz, nstep),
        in_specs=[xspec, xspec, pl.BlockSpec(l2.shape, lambda b, i: (0, 0))],
        out_specs=pl.BlockSpec((None, m, DFT_STEP, W_B), lambda b, i: (b, 0, i, 0)),
        out_shape=jax.ShapeDtypeStruct((bsz, m, m, W_B), BF16),
        compiler_params=_params(("parallel", "parallel")),
        name="dft_stage2",
    )(yr, yi, l2)
    return z.reshape(bsz, n, W_B)


def _mix_out_kernel(o_ref, z_ref, ga_ref, gb_ref, x_ref, mod_ref, wa_ref, wb_ref, wo_ref, out_ref):
    ya = jnp.dot(o_ref[...], wa_ref[...], preferred_element_type=F32)
    yb = jnp.dot(z_ref[...], wb_ref[...], preferred_element_type=F32)
    merged = (jax.nn.sigmoid(ga_ref[...].astype(F32)) * ya
              + jax.nn.sigmoid(gb_ref[...].astype(F32)) * yb)
    m = jnp.dot(merged.astype(BF16), wo_ref[...], preferred_element_type=F32)
    out_ref[...] = x_ref[...] + mod_ref[2:3, :] * m


def mix_out(o, p_gate, z, x, mod, w_a_bf, w_b_bf, w_out_bf, layer, row0):
    bsz, n, _ = x.shape
    tm = min(TM_MIX, n)

    def tok(width, cblk=0):
        return pl.BlockSpec((None, tm, width), lambda b, i: (b, i, cblk))

    def whole(rows, cols):
        return pl.BlockSpec((None, rows, cols), lambda b, i: (layer, 0, 0),
                            pipeline_mode=pl.Buffered(1))

    in_specs = [tok(W_A), tok(W_B), tok(D_MODEL, GCOL_GA // D_MODEL), tok(D_MODEL, GCOL_GB // D_MODEL),
                tok(D_MODEL)]
    args = [o, z, p_gate, p_gate, x]
    in_specs += [
        pl.BlockSpec((None, None, 6, D_MODEL), lambda b, i: (layer, row0 + b, 0, 0)),
        whole(W_A, D_MODEL), whole(W_B, D_MODEL), whole(D_MODEL, D_MODEL),
    ]
    args += [mod, w_a_bf, w_b_bf, w_out_bf]
    return pl.pallas_call(
        _mix_out_kernel,
        grid=(bsz, n // tm),
        in_specs=in_specs,
        out_specs=tok(D_MODEL),
        out_shape=jax.ShapeDtypeStruct((bsz, n, D_MODEL), F32),
        compiler_params=_params(("parallel", "parallel")),
        name="mix_out",
    )(*args)


def _ffn_kernel(*refs, has_final):
    if has_final:
        x_ref, mod_ref, n_ref, w1a_ref, w1g_ref, w2_ref, nf_ref, out_ref, h_ref = refs
    else:
        x_ref, mod_ref, n_ref, w1a_ref, w1g_ref, w2_ref, out_ref, h_ref = refs
    j = pl.program_id(2)

    @pl.when(j == 0)
    def _():
        h = _rms(x_ref[...]) * n_ref[...]
        h = h * (1.0 + mod_ref[4:5, :]) + mod_ref[3:4, :]
        h_ref[...] = h.astype(BF16)
        out_ref[...] = jnp.zeros_like(out_ref)

    h = h_ref[...]
    a = jnp.dot(h, w1a_ref[...], preferred_element_type=F32)
    gt = jnp.dot(h, w1g_ref[...], preferred_element_type=F32)
    act = (_silu(a) * gt).astype(BF16)
    out_ref[...] += jnp.dot(act, w2_ref[...], preferred_element_type=F32)

    @pl.when(j == pl.num_programs(2) - 1)
    def _():
        y = x_ref[...] + mod_ref[5:6, :] * out_ref[...]
        if has_final:
            y = _rms(y) * nf_ref[...]
        out_ref[...] = y


def ffn(x, mod, norm_w, w_ff_in_bf, w_ff_out_bf, norm_final, layer, row0):
    bsz, n, _ = x.shape
    tm = min(TM_FFN, n)
    nj = D_FF // TN_FFN
    has_final = norm_final is not None
    in_specs = [
        pl.BlockSpec((None, tm, D_MODEL), lambda b, i, j: (b, i, 0)),
        pl.BlockSpec((None, None, 6, D_MODEL), lambda b, i, j: (layer, row0 + b, 0, 0)),
        pl.BlockSpec((None, 1, D_MODEL), lambda b, i, j: (layer, 0, 0)),
        pl.BlockSpec((None, D_MODEL, TN_FFN), lambda b, i, j: (layer, 0, j)),
        pl.BlockSpec((None, D_MODEL, TN_FFN), lambda b, i, j: (layer, 0, nj + j)),
        pl.BlockSpec((None, TN_FFN, D_MODEL), lambda b, i, j: (layer, j, 0)),
    ]
    args = [x, mod, norm_w, w_ff_in_bf, w_ff_in_bf, w_ff_out_bf]
    if has_final:
        in_specs.append(pl.BlockSpec((1, D_MODEL), lambda b, i, j: (0, 0)))
        args.append(norm_final)
    return pl.pallas_call(
        functools.partial(_ffn_kernel, has_final=has_final),
        grid=(bsz, n // tm, nj),
        in_specs=in_specs,
        out_specs=pl.BlockSpec((None, tm, D_MODEL), lambda b, i, j: (b, i, 0)),
        out_shape=jax.ShapeDtypeStruct((bsz, n, D_MODEL), F32),
        scratch_shapes=[pltpu.VMEM((tm, D_MODEL), BF16)],
        compiler_params=_params(("parallel", "parallel", "arbitrary")),
        name="ffn",
    )(*args)


def grid_pos_embed(n):
    rows = n // GRID_W
    quarter = D_MODEL // 4
    omega = 1.0 / (POS_BASE ** (jnp.arange(quarter, dtype=F32) / quarter))

    def enc(count):
        a = jnp.arange(count, dtype=F32)[:, None] * omega[None, :]
        return jnp.concatenate([jnp.sin(a), jnp.cos(a)], axis=-1)

    er = jnp.broadcast_to(enc(rows)[:, None, :], (rows, GRID_W, 2 * quarter))
    ec = jnp.broadcast_to(enc(GRID_W)[None, :, :], (rows, GRID_W, 2 * quarter))
    return jnp.concatenate([er, ec], axis=-1).reshape(n, D_MODEL)


def _add_pos_kernel(x_ref, pe_ref, out_ref):
    out_ref[...] = x_ref[...] + pe_ref[...]


def add_pos(x, pe):
    bsz, n, _ = x.shape
    tm = min(TM_IN, n)
    return pl.pallas_call(
        _add_pos_kernel,
        grid=(n // tm, bsz),
        in_specs=[pl.BlockSpec((None, tm, D_MODEL), lambda i, b: (b, i, 0)),
                  pl.BlockSpec((tm, D_MODEL), lambda i, b: (i, 0))],
        out_specs=pl.BlockSpec((None, tm, D_MODEL), lambda i, b: (b, i, 0)),
        out_shape=jax.ShapeDtypeStruct(x.shape, x.dtype),
        compiler_params=_params(("parallel", "parallel")),
        name="add_pos",
    )(x, pe)


def kernel(x_prompt, x_sample, state_hgrn, c, c_ctx, w_mod, b_mod, norm_mix, norm_ffn, w_in, lb_raw,
           g_norm, w_a, w_b, w_out, w_ff_in, w_ff_out, norm_final):
    bc, nc, _ = x_prompt.shape
    bl, nl, _ = x_sample.shape
    assert 1 + bl <= MOD_ROWS

    cvec = jnp.zeros((MOD_ROWS, D_MODEL), F32).at[0].set(c_ctx).at[1:1 + bl].set(c)
    mod = adaln_table(cvec, w_mod, b_mod)

    w_in_bf, w_a_bf, w_b_bf, w_out_bf = (w.astype(BF16) for w in (w_in, w_a, w_b, w_out))
    w_ff_in_bf, w_ff_out_bf = w_ff_in.astype(BF16), w_ff_out.astype(BF16)
    norm_mix3 = norm_mix.reshape(DEPTH, 1, D_MODEL)
    norm_ffn3 = norm_ffn.reshape(DEPTH, 1, D_MODEL)
    g_norm3 = g_norm.reshape(DEPTH, 1, DV)
    norm_final2 = norm_final.reshape(1, D_MODEL)
    lb_dir = jnp.transpose(lb_raw, (1, 0, 2))

    fidx = jnp.arange(FG, dtype=jnp.int32)
    fang = ((fidx[:, None] * fidx[None, :]) % FG).astype(F32) * (2.0 * math.pi / FG)
    cs_f = jnp.concatenate([jnp.cos(fang), jnp.sin(fang)], axis=1).astype(BF16)
    tables = {n: dft2_constants(n) if n == DFT_RADIX ** 2 else dft_tables(n) for n in {nc, nl}}

    def layer(x, seq_shape, state, sbuf, l, row0):
        bt, nt, _ = x.shape
        bs, ns = seq_shape
        p_rec, p_gate = proj_in(x, mod, norm_mix3, w_in_bf, l, row0)
        rec_s = p_rec.reshape(bs, ns, N_REC)
        gate_s = p_gate.reshape(bs, ns, N_GATE)
        o_f, s_f = scan(rec_s, lb_dir, state, l, 0, state_buf=sbuf)
        if sbuf is not None:
            sbuf = s_f
        o, s_b = scan(rec_s, lb_dir, state, l, 1, gate=(o_f, gate_s, g_norm3), state_buf=sbuf)
        if sbuf is not None:
            sbuf = s_b
        if ns == DFT_RADIX ** 2:
            z = fourier_two_stage(gate_s, cs_f, tables[ns])
        else:
            fa, fb = chan_dft(gate_s, cs_f)
            z = seq_dft(fa, fb, *tables[ns])
        x = mix_out(o.reshape(bt, nt, W_A), p_gate, z.reshape(bt, nt, W_B),
                    x, mod, w_a_bf, w_b_bf, w_out_bf, l, row0)
        x = ffn(x, mod, norm_ffn3, w_ff_in_bf, w_ff_out_bf,
                norm_final2 if l == DEPTH - 1 else None, l, row0)
        return x, sbuf

    xc = x_prompt.reshape(1, bc * nc, D_MODEL)
    xs = add_pos(x_sample, grid_pos_embed(nl))
    state_new = jnp.zeros((bc, DEPTH, 2, H_A, DK, DV), F32)
    for l in range(DEPTH):
        xc, state_new = layer(xc, (bc, nc), None, state_new, l, 0)
        xs, _ = layer(xs, (bl, nl), state_hgrn, None, l, 1)
    return xc.reshape(bc, nc, D_MODEL), xs, state_new.astype(x_prompt.dtype)
```

```python
import functools
import math

import jax
import jax.numpy as jnp
from jax import lax
from jax.experimental import pallas as pl
from jax.experimental.pallas import tpu as pltpu

D_MODEL = 2048
DEPTH = 2
H_A = 8
DK = 128
DV = 128
W_A = H_A * DK
N_FG = 4
FG = 256
W_B = N_FG * FG
D_FF = 5632
N_IN = 5 * W_A + W_B + 2 * D_MODEL
EPS = 1e-6
GRID_W = 64
POS_BASE = 10000.0

N_REC = 4 * W_A
N_GATE = N_IN - N_REC
COL_Q = 0
COL_FF = W_A
COL_FB = 2 * W_A
COL_IV = 3 * W_A
GCOL_OG = 0
GCOL_U = W_A
GCOL_GA = W_A + W_B
GCOL_GB = W_A + W_B + D_MODEL

MOD_ROWS = 8
LANES = 128
SUBLANES = 8
VMEM_LIMIT = 56 * 1024 * 1024

TM_IN, TN_IN = 1024, 1024
PROLOGUE_ROWS = 256
TM_MIX = 256
TM_FFN, TN_FFN = 1024, 256
TM_CDFT = 512
TK_SDFT = 256
SCAN_ROWS = 256
SCAN_CHUNK = 128
SCAN_LEVELS = (64, 32, 16, 8)
SCAN_HEADS = 4
LOG2E = 1.4426950408889634

F32 = jnp.float32
BF16 = jnp.bfloat16


def _params(sem):
    return pltpu.CompilerParams(dimension_semantics=sem, vmem_limit_bytes=VMEM_LIMIT)


def _silu(x):
    return x * jax.nn.sigmoid(x)


def _rms(x):
    return x * lax.rsqrt(jnp.mean(x * x, axis=-1, keepdims=True) + EPS)


def _adaln_kernel(c_ref, w_ref, b_ref, out_ref):
    s = _silu(c_ref[...]).astype(BF16)
    out_ref[...] = jnp.dot(s, w_ref[...].astype(BF16), preferred_element_type=F32) + b_ref[...]


def adaln_table(cvec, w_mod, b_mod):
    tn = 1024
    out = pl.pallas_call(
        _adaln_kernel,
        grid=(DEPTH, 6 * D_MODEL // tn),
        in_specs=[
            pl.BlockSpec((MOD_ROWS, D_MODEL), lambda l, j: (0, 0)),
            pl.BlockSpec((None, D_MODEL, tn), lambda l, j: (l, 0, j)),
            pl.BlockSpec((None, 1, tn), lambda l, j: (l, 0, j)),
        ],
        out_specs=pl.BlockSpec((None, MOD_ROWS, tn), lambda l, j: (l, 0, j)),
        out_shape=jax.ShapeDtypeStruct((DEPTH, MOD_ROWS, 6 * D_MODEL), F32),
        compiler_params=_params(("parallel", "parallel")),
        name="adaln",
    )(cvec, w_mod, b_mod.reshape(DEPTH, 1, 6 * D_MODEL))
    return out.reshape(DEPTH, MOD_ROWS, 6, D_MODEL)


def _proj_in_kernel(x_ref, mod_ref, n_ref, w_ref, rec_ref, gate_ref, h_ref):
    j = pl.program_id(2)
    n_rec = N_REC // w_ref.shape[1]
    rows = x_ref.shape[0]
    chunk = min(PROLOGUE_ROWS, rows)

    @pl.when(j == 0)
    def _():
        for r0 in range(0, rows, chunk):
            rs = pl.ds(r0, chunk)
            h = _rms(x_ref[rs, :]) * n_ref[...]
            h = (h * (1.0 + mod_ref[1:2, :]) + mod_ref[0:1, :]).astype(BF16)
            h_ref[rs, :] = h
            rec_ref[rs, :] = jnp.dot(h, w_ref[...], preferred_element_type=F32)

    @pl.when((j > 0) & (j < n_rec))
    def _():
        rec_ref[...] = jnp.dot(h_ref[...], w_ref[...], preferred_element_type=F32)

    @pl.when(j >= n_rec)
    def _():
        gate_ref[...] = jnp.dot(h_ref[...], w_ref[...], preferred_element_type=F32).astype(BF16)


def proj_in(x, mod, norm_w, w_in_bf, layer, row0):
    bsz, n, _ = x.shape
    tm = min(TM_IN, n)
    tn = TN_IN
    n_rec = N_REC // tn
    return pl.pallas_call(
        _proj_in_kernel,
        grid=(bsz, n // tm, N_IN // tn),
        in_specs=[
            pl.BlockSpec((None, tm, D_MODEL), lambda b, i, j: (b, i, 0)),
            pl.BlockSpec((None, None, 6, D_MODEL), lambda b, i, j: (layer, row0 + b, 0, 0)),
            pl.BlockSpec((None, 1, D_MODEL), lambda b, i, j: (layer, 0, 0)),
            pl.BlockSpec((None, D_MODEL, tn), lambda b, i, j: (layer, 0, j)),
        ],
        out_specs=[
            pl.BlockSpec((None, tm, tn), lambda b, i, j: (b, i, jnp.minimum(j, n_rec - 1))),
            pl.BlockSpec((None, tm, tn), lambda b, i, j: (b, i, jnp.maximum(j - n_rec, 0))),
        ],
        out_shape=[jax.ShapeDtypeStruct((bsz, n, N_REC), F32),
                   jax.ShapeDtypeStruct((bsz, n, N_GATE), BF16)],
        scratch_shapes=[pltpu.VMEM((tm, D_MODEL), BF16)],
        compiler_params=_params(("parallel", "parallel", "arbitrary")),
        name="proj_in",
    )(x, mod, norm_w, w_in_bf)


def _nt(a, b):
    return lax.dot_general(a, b, (((1,), (1,)), ((), ())), preferred_element_type=F32)


def _tn(a, b):
    return lax.dot_general(a, b, (((0,), (0,)), ((), ())), preferred_element_type=F32)


def _chunk(q_ref, k_ref, b_ref, v_ref, base, sl, st, reverse, level_masks, row_masks):
    def rows(ref, r0, n):
        return ref[pl.ds(base + r0, n), sl]

    def row(ref, r):
        return ref[pl.ds(base + r, 1), sl]

    q = rows(q_ref, 0, SCAN_CHUNK)
    k = rows(k_ref, 0, SCAN_CHUNK)
    b = rows(b_ref, 0, SCAN_CHUNK)
    v = rows(v_ref, 0, SCAN_CHUNK)
    vb = v.astype(BF16)
    b_end = row(b_ref, 0 if reverse else SCAN_CHUNK - 1)

    o = _nt((q * jnp.exp2(b)).astype(BF16), st.astype(BF16))

    a = jnp.zeros((SCAN_CHUNK, SCAN_CHUNK), F32)
    for h, m in zip(SCAN_LEVELS, level_masks):
        z = jnp.zeros((h, LANES), F32)
        qp, kp = [], []
        for r0 in range(0, SCAN_CHUNK, 2 * h):
            if reverse:
                beta = row(b_ref, r0 + h)
                qp += [rows(q_ref, r0, h) * jnp.exp2(rows(b_ref, r0, h) - beta), z]
                kp += [z, rows(k_ref, r0 + h, h) * jnp.exp2(beta - rows(b_ref, r0 + h, h))]
            else:
                beta = row(b_ref, r0 + h - 1)
                kp += [rows(k_ref, r0, h) * jnp.exp2(beta - rows(b_ref, r0, h)), z]
                qp += [z, rows(q_ref, r0 + h, h) * jnp.exp2(rows(b_ref, r0 + h, h) - beta)]
        p = _nt(jnp.concatenate(qp, axis=0).astype(BF16), jnp.concatenate(kp, axis=0).astype(BF16))
        a = jnp.where(m, p, a)
    o = o + jnp.dot(a.astype(BF16), vb, preferred_element_type=F32)

    parts = []
    for r0 in range(0, SCAN_CHUNK, SUBLANES):
        qv = rows(q_ref, r0, SUBLANES)
        bv = rows(b_ref, r0, SUBLANES)
        acc = jnp.zeros((SUBLANES, LANES), F32)
        for j in range(SUBLANES):
            shape = (SUBLANES, LANES)
            kj = jnp.broadcast_to(row(k_ref, r0 + j), shape)
            bj = jnp.broadcast_to(row(b_ref, r0 + j), shape)
            vj = jnp.broadcast_to(row(v_ref, r0 + j), shape)
            d = bv - bj
            if row_masks[j] is not None:
                d = jnp.where(row_masks[j], d, -1e30)
            pr = qv * kj * jnp.exp2(d)
            acc = acc + jnp.sum(pr, axis=1, keepdims=True) * vj
        parts.append(acc)
    o = o + jnp.concatenate(parts, axis=0)

    khat = (k * jnp.exp2(b_end - b)).astype(BF16)
    st = st * jnp.exp2(b_end) + _tn(vb, khat)
    return o, st


def _scan_kernel(*refs, layer, reverse, has_s0, has_gate, has_sbuf):
    refs = list(refs)
    q_ref, x_ref, v_ref, lb_ref = refs[:4]
    del refs[:4]
    s0_ref = refs.pop(0) if has_s0 else None
    of_ref, og_ref, gn_ref = (refs.pop(0), refs.pop(0), refs.pop(0)) if has_gate else (None, None, None)
    if has_sbuf:
        refs.pop(0)
    o_ref, sfin_ref, st_ref, k_scr, b_scr = refs
    i = pl.program_id(2)
    nch = q_ref.shape[0] // SCAN_CHUNK
    width = q_ref.shape[1]

    @pl.when(i == 0)
    def _():
        for hh in range(SCAN_HEADS):
            if has_s0:
                st_ref[hh] = s0_ref[hh].T
            else:
                st_ref[hh] = jnp.zeros((DV, DK), F32)

    t_idx = lax.broadcasted_iota(jnp.int32, (SCAN_CHUNK, SCAN_CHUNK), 0)
    s_idx = lax.broadcasted_iota(jnp.int32, (SCAN_CHUNK, SCAN_CHUNK), 1)
    incl = (s_idx >= t_idx) if reverse else (s_idx <= t_idx)
    strict = (s_idx > t_idx) if reverse else (s_idx < t_idx)
    tmat = jnp.where(incl, 1.0, 0.0).astype(BF16)
    diff = t_idx ^ s_idx
    level_masks = [(diff >= h) & (diff < 2 * h) & strict for h in SCAN_LEVELS]
    sub = lax.broadcasted_iota(jnp.int32, (SUBLANES, LANES), 0)
    if reverse:
        row_masks = [(sub <= j) if j < SUBLANES - 1 else None for j in range(SUBLANES)]
    else:
        row_masks = [(sub >= j) if j > 0 else None for j in range(SUBLANES)]

    lbr = lb_ref[...]
    e = jnp.exp(lbr - jnp.max(lbr, axis=0, keepdims=True))
    lb_all = jnp.sum(e[1:layer + 1], axis=0, keepdims=True) / jnp.sum(e, axis=0, keepdims=True) \
        if layer > 0 else jnp.zeros((1, width), F32)

    for c in (range(nch - 1, -1, -1) if reverse else range(nch)):
        base = SCAN_CHUNK * c
        rs = pl.ds(base, SCAN_CHUNK)
        f = lb_all + (1.0 - lb_all) * jax.nn.sigmoid(x_ref[rs, :])
        g = jnp.log(f) * LOG2E
        ghi = g.astype(BF16)
        glo = (g - ghi.astype(F32)).astype(BF16)
        b2 = jnp.dot(tmat, jnp.concatenate([ghi, glo], axis=1), preferred_element_type=F32)
        k_scr[rs, :] = 1.0 - f
        b_scr[rs, :] = b2[:, :width] + b2[:, width:]
        for hh in range(SCAN_HEADS):
            sl = slice(LANES * hh, LANES * (hh + 1))
            o, st = _chunk(q_ref, k_scr, b_scr, v_ref, base, sl, st_ref[hh], reverse,
                           level_masks, row_masks)
            if has_gate:
                o = _rms(o + of_ref[rs, sl]) * gn_ref[...]
                o_ref[rs, sl] = (o * _silu(og_ref[rs, sl].astype(F32))).astype(BF16)
            else:
                o_ref[rs, sl] = o
            st_ref[hh] = st

    @pl.when(i == pl.num_programs(2) - 1)
    def _():
        for hh in range(SCAN_HEADS):
            sfin_ref[hh] = st_ref[hh].T


def scan(p, lb_dir, state, layer, direction, gate=None, state_buf=None):
    bsz, n, _ = p.shape
    reverse = direction == 1
    nblk = n // SCAN_ROWS
    wblk = LANES * SCAN_HEADS
    has_s0 = state is not None
    has_gate = gate is not None
    has_sbuf = state_buf is not None

    def rowblk(i):
        return nblk - 1 - i if reverse else i

    def col(c0):
        return lambda b, h, i: (b, rowblk(i), c0 // wblk + h)

    in_specs = [
        pl.BlockSpec((None, SCAN_ROWS, wblk), col(COL_Q)),
        pl.BlockSpec((None, SCAN_ROWS, wblk), col(COL_FB if reverse else COL_FF)),
        pl.BlockSpec((None, SCAN_ROWS, wblk), col(COL_IV)),
        pl.BlockSpec((None, DEPTH, wblk), lambda b, h, i: (direction, 0, h)),
    ]
    args = [p, p, p, lb_dir]
    if has_s0:
        in_specs.append(pl.BlockSpec((None, None, None, SCAN_HEADS, DK, DV),
                                     lambda b, h, i: (b, layer, direction, h, 0, 0)))
        args.append(state)
    if has_gate:
        o_other, p_gate, g_norm = gate
        in_specs += [
            pl.BlockSpec((None, SCAN_ROWS, wblk), lambda b, h, i: (b, rowblk(i), h)),
            pl.BlockSpec((None, SCAN_ROWS, wblk), lambda b, h, i: (b, rowblk(i), GCOL_OG // wblk + h)),
            pl.BlockSpec((None, 1, DV), lambda b, h, i: (layer, 0, 0)),
        ]
        args += [o_other, p_gate, g_norm]
    aliases = {}
    if has_sbuf:
        aliases = {len(args): 1}
        in_specs.append(pl.BlockSpec(memory_space=pl.ANY))
        args.append(state_buf)
        s_spec = pl.BlockSpec((None, None, None, SCAN_HEADS, DK, DV),
                              lambda b, h, i: (b, layer, direction, h, 0, 0))
        s_shape = jax.ShapeDtypeStruct(state_buf.shape, F32)
    else:
        s_spec = pl.BlockSpec((None, SCAN_HEADS, DK, DV), lambda b, h, i: (b, h, 0, 0))
        s_shape = jax.ShapeDtypeStruct((bsz, H_A, DK, DV), F32)
    return pl.pallas_call(
        functools.partial(_scan_kernel, layer=layer, reverse=reverse, has_s0=has_s0,
                          has_gate=has_gate, has_sbuf=has_sbuf),
        grid=(bsz, H_A // SCAN_HEADS, nblk),
        in_specs=in_specs,
        out_specs=[
            pl.BlockSpec((None, SCAN_ROWS, wblk), lambda b, h, i: (b, rowblk(i), h)),
            s_spec,
        ],
        out_shape=[
            jax.ShapeDtypeStruct((bsz, n, W_A), BF16 if has_gate else F32),
            s_shape,
        ],
        input_output_aliases=aliases,
        scratch_shapes=[pltpu.VMEM((SCAN_HEADS, DV, DK), F32),
                        pltpu.VMEM((SCAN_ROWS, wblk), F32),
                        pltpu.VMEM((SCAN_ROWS, wblk), F32)],
        compiler_params=_params(("parallel", "parallel", "arbitrary")),
        name="scan_bwd" if reverse else "scan_fwd",
    )(*args)


def _chan_dft_kernel(u_ref, cs_ref, a_ref, b_ref):
    cs = cs_ref[...]
    for g in range(N_FG):
        sl = slice(FG * g, FG * (g + 1))
        r = jnp.dot(u_ref[:, sl], cs, preferred_element_type=F32)
        a_ref[:, sl] = r[:, :FG].astype(BF16)
        b_ref[:, sl] = r[:, FG:].astype(BF16)


def chan_dft(p, cs_f):
    bsz, n, _ = p.shape
    tm = min(TM_CDFT, n)
    out = jax.ShapeDtypeStruct((bsz, n, W_B), BF16)
    spec = pl.BlockSpec((None, tm, W_B), lambda b, i: (b, i, 0))
    return pl.pallas_call(
        _chan_dft_kernel,
        grid=(bsz, n // tm),
        in_specs=[
            pl.BlockSpec((None, tm, W_B), lambda b, i: (b, i, GCOL_U // W_B)),
            pl.BlockSpec((FG, 2 * FG), lambda b, i: (0, 0)),
        ],
        out_specs=[spec, spec],
        out_shape=[out, out],
        compiler_params=_params(("parallel", "parallel")),
        name="chan_dft",
    )(p, cs_f)


def _seq_dft_kernel(c_ref, s_ref, a_ref, b_ref, z_ref, *, scale):
    acc = jnp.dot(c_ref[...], a_ref[...], preferred_element_type=F32)
    acc = acc - jnp.dot(s_ref[...], b_ref[...], preferred_element_type=F32)
    z_ref[...] = (acc * scale).astype(BF16)


def seq_dft(a, b, cos_n, sin_n):
    bsz, n, _ = a.shape
    tk = min(TK_SDFT, n)
    full = pl.BlockSpec((None, n, W_B), lambda bi, i: (bi, 0, 0))
    slab = pl.BlockSpec((tk, n), lambda bi, i: (i, 0))
    return pl.pallas_call(
        functools.partial(_seq_dft_kernel, scale=1.0 / math.sqrt(n * FG)),
        grid=(bsz, n // tk),
        in_specs=[slab, slab, full, full],
        out_specs=pl.BlockSpec((None, tk, W_B), lambda bi, i: (bi, i, 0)),
        out_shape=jax.ShapeDtypeStruct((bsz, n, W_B), BF16),
        compiler_params=_params(("parallel", "parallel")),
        name="seq_dft",
    )(cos_n, sin_n, a, b)


def dft_tables(n):
    idx = jnp.arange(n, dtype=jnp.int32)
    ang = ((idx[:, None] * idx[None, :]) % n).astype(F32) * (2.0 * math.pi / n)
    return jnp.cos(ang).astype(BF16), jnp.sin(ang).astype(BF16)


DFT_RADIX = 64
DFT_GROUP = 8
DFT_STEP = 16


def _kron_rows(mat, inner):
    r, c = mat.shape
    eye = jnp.eye(inner, dtype=mat.dtype)
    return (mat[:, None, :, None] * eye[None, :, None, :]).reshape(r * inner, c * inner)


def dft2_constants(n):
    m = DFT_RADIX
    idx = jnp.arange(m, dtype=jnp.int32)
    ang = ((idx[:, None] * idx[None, :]) % m).astype(F32) * (2.0 * math.pi / m)
    cm, sm = jnp.cos(ang), jnp.sin(ang)
    ck, sk = _kron_rows(cm, DFT_GROUP), _kron_rows(sm, DFT_GROUP)
    l1 = jnp.concatenate([jnp.concatenate([ck, -sk], axis=1),
                          jnp.concatenate([sk, ck], axis=1)], axis=0).astype(BF16)
    eye = jnp.eye(DFT_GROUP, dtype=F32)
    c2 = (cm[:, None, None, :] * eye[None, :, :, None]).reshape(m * DFT_GROUP, DFT_GROUP * m)
    s2 = (sm[:, None, None, :] * eye[None, :, :, None]).reshape(m * DFT_GROUP, DFT_GROUP * m)
    l2 = jnp.concatenate([c2, -s2], axis=1).astype(BF16)
    k1 = jnp.arange(m, dtype=jnp.int32)[None, :, None]
    n2 = (jnp.arange(m // DFT_GROUP, dtype=jnp.int32)[:, None, None] * DFT_GROUP
          + jnp.arange(DFT_GROUP, dtype=jnp.int32)[None, None, :])
    tang = ((k1 * n2) % n).astype(F32) * (2.0 * math.pi / n)
    tang = jnp.broadcast_to(tang.reshape(m // DFT_GROUP, m * DFT_GROUP, 1),
                            (m // DFT_GROUP, m * DFT_GROUP, LANES))
    return l1, l2, jnp.cos(tang), jnp.sin(tang)


def _dft_stage1_kernel(u_ref, cs_ref, l1_ref, tc_ref, ts_ref, yr_ref, yi_ref):
    m = DFT_RADIX
    rows = m * DFT_GROUP
    cs = cs_ref[...]
    l1 = l1_ref[...]
    yr_parts, yi_parts = [], []
    u_all = u_ref[...].astype(F32)
    for hf in range(DFT_STEP // DFT_GROUP):
        u = u_all[:, DFT_GROUP * hf:DFT_GROUP * (hf + 1), :].reshape(rows, W_B)
        a_parts, b_parts = [], []
        for g in range(N_FG):
            r = jnp.dot(u[:, FG * g:FG * (g + 1)].astype(BF16), cs, preferred_element_type=F32)
            a_parts.append(r[:, :FG].astype(BF16))
            b_parts.append(r[:, FG:].astype(BF16))
        x = jnp.concatenate([jnp.concatenate(a_parts, axis=1), jnp.concatenate(b_parts, axis=1)], axis=0)
        y = jnp.dot(l1, x, preferred_element_type=F32)
        yr, yi = y[:rows], y[rows:]
        tc, ts = tc_ref[hf], ts_ref[hf]
        pr, pi = [], []
        for c in range(W_B // LANES):
            sl = slice(LANES * c, LANES * (c + 1))
            pr.append(yr[:, sl] * tc - yi[:, sl] * ts)
            pi.append(yr[:, sl] * ts + yi[:, sl] * tc)
        yr_parts.append(jnp.concatenate(pr, axis=1).reshape(m, DFT_GROUP, W_B))
        yi_parts.append(jnp.concatenate(pi, axis=1).reshape(m, DFT_GROUP, W_B))
    yr_ref[...] = jnp.concatenate(yr_parts, axis=1).astype(BF16)
    yi_ref[...] = jnp.concatenate(yi_parts, axis=1).astype(BF16)


def _dft_stage2_kernel(yr_ref, yi_ref, l2_ref, z_ref, *, scale):
    m = DFT_RADIX
    rows = m * DFT_GROUP
    l2 = l2_ref[...]
    parts = []
    for hf in range(DFT_STEP // DFT_GROUP):
        sl = slice(DFT_GROUP * hf, DFT_GROUP * (hf + 1))
        x = jnp.concatenate([yr_ref[sl].reshape(rows, W_B), yi_ref[sl].reshape(rows, W_B)], axis=0)
        z = jnp.dot(l2, x, preferred_element_type=F32) * scale
        parts.append(z.reshape(m, DFT_GROUP, W_B))
    z_ref[...] = jnp.concatenate(parts, axis=1).astype(BF16)


def fourier_two_stage(p, cs_f, consts):
    bsz, n, _ = p.shape
    m = DFT_RADIX
    l1, l2, tw_c, tw_s = consts
    p4 = p.reshape(bsz, m, m, N_GATE)
    nstep = m // DFT_STEP
    per_step = DFT_STEP // DFT_GROUP
    ybuf = jax.ShapeDtypeStruct((bsz, m, m, W_B), BF16)
    yspec = pl.BlockSpec((None, m, DFT_STEP, W_B), lambda b, j: (b, 0, j, 0))
    twspec = pl.BlockSpec((per_step, m * DFT_GROUP, LANES), lambda b, j: (j, 0, 0))
    yr, yi = pl.pallas_call(
        _dft_stage1_kernel,
        grid=(bsz, nstep),
        in_specs=[
            pl.BlockSpec((None, m, DFT_STEP, W_B), lambda b, j: (b, 0, j, GCOL_U // W_B)),
            pl.BlockSpec((FG, 2 * FG), lambda b, j: (0, 0)),
            pl.BlockSpec(l1.shape, lambda b, j: (0, 0)),
            twspec, twspec,
        ],
        out_specs=[yspec, yspec],
        out_shape=[ybuf, ybuf],
        compiler_params=_params(("parallel", "parallel")),
        name="dft_stage1",
    )(p4, cs_f, l1, tw_c, tw_s)
    xspec = pl.BlockSpec((None, DFT_STEP, m, W_B), lambda b, i: (b, i, 0, 0))
    z = pl.pallas_call(
        functools.partial(_dft_stage2_kernel, scale=1.0 / math.sqrt(n * FG)),
        grid=(bsz, nstep),
        in_specs=[xspec, xspec, pl.BlockSpec(l2.shape, lambda b, i: (0, 0))],
        out_specs=pl.BlockSpec((None, m, DFT_STEP, W_B), lambda b, i: (b, 0, i, 0)),
        out_shape=jax.ShapeDtypeStruct((bsz, m, m, W_B), BF16),
        compiler_params=_params(("parallel", "parallel")),
        name="dft_stage2",
    )(yr, yi, l2)
    return z.reshape(bsz, n, W_B)


def _mix_out_kernel(o_ref, z_ref, ga_ref, gb_ref, x_ref, mod_ref, wa_ref, wb_ref, wo_ref, out_ref):
    ya = jnp.dot(o_ref[...], wa_ref[...], preferred_element_type=F32)
    yb = jnp.dot(z_ref[...], wb_ref[...], preferred_element_type=F32)
    merged = (jax.nn.sigmoid(ga_ref[...].astype(F32)) * ya
              + jax.nn.sigmoid(gb_ref[...].astype(F32)) * yb)
    m = jnp.dot(merged.astype(BF16), wo_ref[...], preferred_element_type=F32)
    out_ref[...] = x_ref[...] + mod_ref[2:3, :] * m


def mix_out(o, p_gate, z, x, mod, w_a_bf, w_b_bf, w_out_bf, layer, row0):
    bsz, n, _ = x.shape
    tm = min(TM_MIX, n)

    def tok(width, cblk=0):
        return pl.BlockSpec((None, tm, width), lambda b, i: (b, i, cblk))

    def whole(rows, cols):
        return pl.BlockSpec((None, rows, cols), lambda b, i: (layer, 0, 0),
                            pipeline_mode=pl.Buffered(1))

    in_specs = [tok(W_A), tok(W_B), tok(D_MODEL, GCOL_GA // D_MODEL), tok(D_MODEL, GCOL_GB // D_MODEL),
                tok(D_MODEL)]
    args = [o, z, p_gate, p_gate, x]
    in_specs += [
        pl.BlockSpec((None, None, 6, D_MODEL), lambda b, i: (layer, row0 + b, 0, 0)),
        whole(W_A, D_MODEL), whole(W_B, D_MODEL), whole(D_MODEL, D_MODEL),
    ]
    args += [mod, w_a_bf, w_b_bf, w_out_bf]
    return pl.pallas_call(
        _mix_out_kernel,
        grid=(bsz, n // tm),
        in_specs=in_specs,
        out_specs=tok(D_MODEL),
        out_shape=jax.ShapeDtypeStruct((bsz, n, D_MODEL), F32),
        compiler_params=_params(("parallel", "parallel")),
        name="mix_out",
    )(*args)


def _ffn_kernel(*refs, has_final):
    if has_final:
        x_ref, mod_ref, n_ref, w1a_ref, w1g_ref, w2_ref, nf_ref, out_ref, h_ref = refs
    else:
        x_ref, mod_ref, n_ref, w1a_ref, w1g_ref, w2_ref, out_ref, h_ref = refs
    j = pl.program_id(2)
    rows = x_ref.shape[0]
    chunk = min(PROLOGUE_ROWS, rows)

    def partial_out(h):
        a = jnp.dot(h, w1a_ref[...], preferred_element_type=F32)
        gt = jnp.dot(h, w1g_ref[...], preferred_element_type=F32)
        return jnp.dot((_silu(a) * gt).astype(BF16), w2_ref[...], preferred_element_type=F32)

    @pl.when(j == 0)
    def _():
        for r0 in range(0, rows, chunk):
            rs = pl.ds(r0, chunk)
            h = _rms(x_ref[rs, :]) * n_ref[...]
            h = (h * (1.0 + mod_ref[4:5, :]) + mod_ref[3:4, :]).astype(BF16)
            h_ref[rs, :] = h
            out_ref[rs, :] = partial_out(h)

    @pl.when(j > 0)
    def _():
        out_ref[...] += partial_out(h_ref[...])

    @pl.when(j == pl.num_programs(2) - 1)
    def _():
        y = x_ref[...] + mod_ref[5:6, :] * out_ref[...]
        if has_final:
            y = _rms(y) * nf_ref[...]
        out_ref[...] = y


def ffn(x, mod, norm_w, w_ff_in_bf, w_ff_out_bf, norm_final, layer, row0):
    bsz, n, _ = x.shape
    tm = min(TM_FFN, n)
    nj = D_FF // TN_FFN
    has_final = norm_final is not None
    in_specs = [
        pl.BlockSpec((None, tm, D_MODEL), lambda b, i, j: (b, i, 0)),
        pl.BlockSpec((None, None, 6, D_MODEL), lambda b, i, j: (layer, row0 + b, 0, 0)),
        pl.BlockSpec((None, 1, D_MODEL), lambda b, i, j: (layer, 0, 0)),
        pl.BlockSpec((None, D_MODEL, TN_FFN), lambda b, i, j: (layer, 0, j)),
        pl.BlockSpec((None, D_MODEL, TN_FFN), lambda b, i, j: (layer, 0, nj + j)),
        pl.BlockSpec((None, TN_FFN, D_MODEL), lambda b, i, j: (layer, j, 0)),
    ]
    args = [x, mod, norm_w, w_ff_in_bf, w_ff_in_bf, w_ff_out_bf]
    if has_final:
        in_specs.append(pl.BlockSpec((1, D_MODEL), lambda b, i, j: (0, 0)))
        args.append(norm_final)
    return pl.pallas_call(
        functools.partial(_ffn_kernel, has_final=has_final),
        grid=(bsz, n // tm, nj),
        in_specs=in_specs,
        out_specs=pl.BlockSpec((None, tm, D_MODEL), lambda b, i, j: (b, i, 0)),
        out_shape=jax.ShapeDtypeStruct((bsz, n, D_MODEL), F32),
        scratch_shapes=[pltpu.VMEM((tm, D_MODEL), BF16)],
        compiler_params=_params(("parallel", "parallel", "arbitrary")),
        name="ffn",
    )(*args)


def grid_pos_embed(n):
    rows = n // GRID_W
    quarter = D_MODEL // 4
    omega = 1.0 / (POS_BASE ** (jnp.arange(quarter, dtype=F32) / quarter))

    def enc(count):
        a = jnp.arange(count, dtype=F32)[:, None] * omega[None, :]
        return jnp.concatenate([jnp.sin(a), jnp.cos(a)], axis=-1)

    er = jnp.broadcast_to(enc(rows)[:, None, :], (rows, GRID_W, 2 * quarter))
    ec = jnp.broadcast_to(enc(GRID_W)[None, :, :], (rows, GRID_W, 2 * quarter))
    return jnp.concatenate([er, ec], axis=-1).reshape(n, D_MODEL)


def _add_pos_kernel(x_ref, pe_ref, out_ref):
    out_ref[...] = x_ref[...] + pe_ref[...]


def add_pos(x, pe):
    bsz, n, _ = x.shape
    tm = min(TM_IN, n)
    return pl.pallas_call(
        _add_pos_kernel,
        grid=(n // tm, bsz),
        in_specs=[pl.BlockSpec((None, tm, D_MODEL), lambda i, b: (b, i, 0)),
                  pl.BlockSpec((tm, D_MODEL), lambda i, b: (i, 0))],
        out_specs=pl.BlockSpec((None, tm, D_MODEL), lambda i, b: (b, i, 0)),
        out_shape=jax.ShapeDtypeStruct(x.shape, x.dtype),
        compiler_params=_params(("parallel", "parallel")),
        name="add_pos",
    )(x, pe)


def kernel(x_prompt, x_sample, state_hgrn, c, c_ctx, w_mod, b_mod, norm_mix, norm_ffn, w_in, lb_raw,
           g_norm, w_a, w_b, w_out, w_ff_in, w_ff_out, norm_final):
    bc, nc, _ = x_prompt.shape
    bl, nl, _ = x_sample.shape
    assert 1 + bl <= MOD_ROWS

    cvec = jnp.zeros((MOD_ROWS, D_MODEL), F32).at[0].set(c_ctx).at[1:1 + bl].set(c)
    mod = adaln_table(cvec, w_mod, b_mod)

    w_in_bf, w_a_bf, w_b_bf, w_out_bf = (w.astype(BF16) for w in (w_in, w_a, w_b, w_out))
    w_ff_in_bf, w_ff_out_bf = w_ff_in.astype(BF16), w_ff_out.astype(BF16)
    norm_mix3 = norm_mix.reshape(DEPTH, 1, D_MODEL)
    norm_ffn3 = norm_ffn.reshape(DEPTH, 1, D_MODEL)
    g_norm3 = g_norm.reshape(DEPTH, 1, DV)
    norm_final2 = norm_final.reshape(1, D_MODEL)
    lb_dir = jnp.transpose(lb_raw, (1, 0, 2))

    fidx = jnp.arange(FG, dtype=jnp.int32)
    fang = ((fidx[:, None] * fidx[None, :]) % FG).astype(F32) * (2.0 * math.pi / FG)
    cs_f = jnp.concatenate([jnp.cos(fang), jnp.sin(fang)], axis=1).astype(BF16)
    tables = {n: dft2_constants(n) if n == DFT_RADIX ** 2 else dft_tables(n) for n in {nc, nl}}

    def layer(x, seq_shape, state, sbuf, l, row0):
        bt, nt, _ = x.shape
        bs, ns = seq_shape
        p_rec, p_gate = proj_in(x, mod, norm_mix3, w_in_bf, l, row0)
        rec_s = p_rec.reshape(bs, ns, N_REC)
        gate_s = p_gate.reshape(bs, ns, N_GATE)
        o_f, s_f = scan(rec_s, lb_dir, state, l, 0, state_buf=sbuf)
        if sbuf is not None:
            sbuf = s_f
        o, s_b = scan(rec_s, lb_dir, state, l, 1, gate=(o_f, gate_s, g_norm3), state_buf=sbuf)
        if sbuf is not None:
            sbuf = s_b
        if ns == DFT_RADIX ** 2:
            z = fourier_two_stage(gate_s, cs_f, tables[ns])
        else:
            fa, fb = chan_dft(gate_s, cs_f)
            z = seq_dft(fa, fb, *tables[ns])
        x = mix_out(o.reshape(bt, nt, W_A), p_gate, z.reshape(bt, nt, W_B),
                    x, mod, w_a_bf, w_b_bf, w_out_bf, l, row0)
        x = ffn(x, mod, norm_ffn3, w_ff_in_bf, w_ff_out_bf,
                norm_final2 if l == DEPTH - 1 else None, l, row0)
        return x, sbuf

    xc = x_prompt.reshape(1, bc * nc, D_MODEL)
    xs = add_pos(x_sample, grid_pos_embed(nl))
    state_new = jnp.zeros((bc, DEPTH, 2, H_A, DK, DV), F32)
    for l in range(DEPTH):
        xc, state_new = layer(xc, (bc, nc), None, state_new, l, 0)
        xs, _ = layer(xs, (bl, nl), state_hgrn, None, l, 1)
    return xc.reshape(bc, nc, D_MODEL), xs, state_new.astype(x_prompt.dtype)
```

```python
import functools
import math

import jax
import jax.numpy as jnp
from jax import lax
from jax.experimental import pallas as pl
from jax.experimental.pallas import tpu as pltpu

D_MODEL = 2048
DEPTH = 2
H_A = 8
DK = 128
DV = 128
W_A = H_A * DK
N_FG = 4
FG = 256
W_B = N_FG * FG
D_FF = 5632
N_IN = 5 * W_A + W_B + 2 * D_MODEL
EPS = 1e-6
GRID_W = 64
POS_BASE = 10000.0

N_REC = 4 * W_A
N_GATE = N_IN - N_REC
COL_Q = 0
COL_FF = W_A
COL_FB = 2 * W_A
COL_IV = 3 * W_A
GCOL_OG = 0
GCOL_U = W_A
GCOL_GA = W_A + W_B
GCOL_GB = W_A + W_B + D_MODEL

MOD_ROWS = 8
LANES = 128
SUBLANES = 8
VMEM_LIMIT = 56 * 1024 * 1024

TM_IN, TN_IN = 1024, 1024
PROLOGUE_ROWS = 256
TM_MIX = 256
TM_FFN, TN_FFN = 1024, 256
TM_CDFT = 512
TK_SDFT = 256
SCAN_ROWS = 256
SCAN_CHUNK = 128
SCAN_LEVELS = (64, 32, 16, 8)
SCAN_HEADS = 8
LOG2E = 1.4426950408889634

F32 = jnp.float32
BF16 = jnp.bfloat16


def _params(sem):
    return pltpu.CompilerParams(dimension_semantics=sem, vmem_limit_bytes=VMEM_LIMIT)


def _silu(x):
    return x * jax.nn.sigmoid(x)


def _rms(x):
    return x * lax.rsqrt(jnp.mean(x * x, axis=-1, keepdims=True) + EPS)


def _adaln_kernel(c_ref, w_ref, b_ref, out_ref):
    s = _silu(c_ref[...]).astype(BF16)
    out_ref[...] = jnp.dot(s, w_ref[...].astype(BF16), preferred_element_type=F32) + b_ref[...]


def adaln_table(cvec, w_mod, b_mod):
    tn = 1024
    out = pl.pallas_call(
        _adaln_kernel,
        grid=(DEPTH, 6 * D_MODEL // tn),
        in_specs=[
            pl.BlockSpec((MOD_ROWS, D_MODEL), lambda l, j: (0, 0)),
            pl.BlockSpec((None, D_MODEL, tn), lambda l, j: (l, 0, j)),
            pl.BlockSpec((None, 1, tn), lambda l, j: (l, 0, j)),
        ],
        out_specs=pl.BlockSpec((None, MOD_ROWS, tn), lambda l, j: (l, 0, j)),
        out_shape=jax.ShapeDtypeStruct((DEPTH, MOD_ROWS, 6 * D_MODEL), F32),
        compiler_params=_params(("parallel", "parallel")),
        name="adaln",
    )(cvec, w_mod, b_mod.reshape(DEPTH, 1, 6 * D_MODEL))
    return out.reshape(DEPTH, MOD_ROWS, 6, D_MODEL)


def _proj_in_kernel(x_ref, mod_ref, n_ref, w_ref, rec_ref, gate_ref, h_ref):
    j = pl.program_id(2)
    n_rec = N_REC // w_ref.shape[1]
    rows = x_ref.shape[0]
    chunk = min(PROLOGUE_ROWS, rows)

    @pl.when(j == 0)
    def _():
        for r0 in range(0, rows, chunk):
            rs = pl.ds(r0, chunk)
            h = _rms(x_ref[rs, :]) * n_ref[...]
            h = (h * (1.0 + mod_ref[1:2, :]) + mod_ref[0:1, :]).astype(BF16)
            h_ref[rs, :] = h
            rec_ref[rs, :] = jnp.dot(h, w_ref[...], preferred_element_type=F32)

    @pl.when((j > 0) & (j < n_rec))
    def _():
        rec_ref[...] = jnp.dot(h_ref[...], w_ref[...], preferred_element_type=F32)

    @pl.when(j >= n_rec)
    def _():
        gate_ref[...] = jnp.dot(h_ref[...], w_ref[...], preferred_element_type=F32).astype(BF16)


def proj_in(x, mod, norm_w, w_in_bf, layer, row0):
    bsz, n, _ = x.shape
    tm = min(TM_IN, n)
    tn = TN_IN
    n_rec = N_REC // tn
    return pl.pallas_call(
        _proj_in_kernel,
        grid=(bsz, n // tm, N_IN // tn),
        in_specs=[
            pl.BlockSpec((None, tm, D_MODEL), lambda b, i, j: (b, i, 0)),
            pl.BlockSpec((None, None, 6, D_MODEL), lambda b, i, j: (layer, row0 + b, 0, 0)),
            pl.BlockSpec((None, 1, D_MODEL), lambda b, i, j: (layer, 0, 0)),
            pl.BlockSpec((None, D_MODEL, tn), lambda b, i, j: (layer, 0, j)),
        ],
        out_specs=[
            pl.BlockSpec((None, tm, tn), lambda b, i, j: (b, i, jnp.minimum(j, n_rec - 1))),
            pl.BlockSpec((None, tm, tn), lambda b, i, j: (b, i, jnp.maximum(j - n_rec, 0))),
        ],
        out_shape=[jax.ShapeDtypeStruct((bsz, n, N_REC), F32),
                   jax.ShapeDtypeStruct((bsz, n, N_GATE), BF16)],
        scratch_shapes=[pltpu.VMEM((tm, D_MODEL), BF16)],
        compiler_params=_params(("parallel", "parallel", "arbitrary")),
        name="proj_in",
    )(x, mod, norm_w, w_in_bf)


def _nt(a, b):
    return lax.dot_general(a, b, (((1,), (1,)), ((), ())), preferred_element_type=F32)


def _tn(a, b):
    return lax.dot_general(a, b, (((0,), (0,)), ((), ())), preferred_element_type=F32)


def _chunk(q_ref, k_ref, b_ref, v_ref, m_ref, base, sl, st, reverse, row_masks, lane_masks):
    def rows(ref, r0, n):
        return ref[pl.ds(base + r0, n), sl]

    def row(ref, r):
        return ref[pl.ds(base + r, 1), sl]

    q = rows(q_ref, 0, SCAN_CHUNK)
    k = rows(k_ref, 0, SCAN_CHUNK)
    b = rows(b_ref, 0, SCAN_CHUNK)
    vb = rows(v_ref, 0, SCAN_CHUNK).astype(BF16)
    b_end = row(b_ref, 0 if reverse else SCAN_CHUNK - 1)

    o = _nt((q * jnp.exp2(b)).astype(BF16), st.astype(BF16))

    levels = []
    for h in SCAN_LEVELS:
        z = jnp.zeros((h, LANES), F32)
        qp, kp = [], []
        for r0 in range(0, SCAN_CHUNK, 2 * h):
            if reverse:
                beta = row(b_ref, r0 + h)
                qp += [rows(q_ref, r0, h) * jnp.exp2(rows(b_ref, r0, h) - beta), z]
                kp += [z, rows(k_ref, r0 + h, h) * jnp.exp2(beta - rows(b_ref, r0 + h, h))]
            else:
                beta = row(b_ref, r0 + h - 1)
                kp += [rows(k_ref, r0, h) * jnp.exp2(beta - rows(b_ref, r0, h)), z]
                qp += [z, rows(q_ref, r0 + h, h) * jnp.exp2(rows(b_ref, r0 + h, h) - beta)]
        levels.append(_nt(jnp.concatenate(qp, axis=0).astype(BF16),
                          jnp.concatenate(kp, axis=0).astype(BF16)))

    parts = []
    for r0 in range(0, SCAN_CHUNK, SUBLANES):
        qv = rows(q_ref, r0, SUBLANES)
        bv = rows(b_ref, r0, SUBLANES)
        near = jnp.zeros((SUBLANES, LANES), F32)
        for j in range(SUBLANES):
            shape = (SUBLANES, LANES)
            kj = jnp.broadcast_to(row(k_ref, r0 + j), shape)
            bj = jnp.broadcast_to(row(b_ref, r0 + j), shape)
            d = bv - bj
            if row_masks[j] is not None:
                d = jnp.where(row_masks[j], d, -1e30)
            w = jnp.sum(qv * kj * jnp.exp2(d), axis=1, keepdims=True)
            near = jnp.where(lane_masks[j], w, near)
        rs = slice(r0, r0 + SUBLANES)
        acc = near * m_ref[len(SCAN_LEVELS), rs, :]
        for li, h in enumerate(SCAN_LEVELS):
            if ((r0 & h) == 0) == reverse:
                acc = acc + levels[li][rs] * m_ref[li, rs, :]
        parts.append(acc)
    a = jnp.concatenate(parts, axis=0)
    o = o + jnp.dot(a.astype(BF16), vb, preferred_element_type=F32)

    khat = (k * jnp.exp2(b_end - b)).astype(BF16)
    st = st * jnp.exp2(b_end) + _tn(vb, khat)
    return o, st


def _scan_kernel(*refs, layer, reverse, has_s0, has_gate, has_sbuf):
    refs = list(refs)
    q_ref, x_ref, v_ref, lb_ref, m_ref = refs[:5]
    del refs[:5]
    s0_ref = refs.pop(0) if has_s0 else None
    of_ref, og_ref, gn_ref = (refs.pop(0), refs.pop(0), refs.pop(0)) if has_gate else (None, None, None)
    if has_sbuf:
        refs.pop(0)
    o_ref, sfin_ref, st_ref, k_scr, b_scr = refs
    i = pl.program_id(2)
    nch = q_ref.shape[0] // SCAN_CHUNK
    width = q_ref.shape[1]

    @pl.when(i == 0)
    def _():
        for hh in range(SCAN_HEADS):
            if has_s0:
                st_ref[hh] = s0_ref[hh].T
            else:
                st_ref[hh] = jnp.zeros((DV, DK), F32)

    t_idx = lax.broadcasted_iota(jnp.int32, (SCAN_CHUNK, SCAN_CHUNK), 0)
    s_idx = lax.broadcasted_iota(jnp.int32, (SCAN_CHUNK, SCAN_CHUNK), 1)
    incl = (s_idx >= t_idx) if reverse else (s_idx <= t_idx)
    tmat = jnp.where(incl, 1.0, 0.0).astype(BF16)
    sub = lax.broadcasted_iota(jnp.int32, (SUBLANES, LANES), 0)
    lane = lax.broadcasted_iota(jnp.int32, (SUBLANES, LANES), 1)
    lane_masks = [(lane & (SUBLANES - 1)) == j for j in range(SUBLANES)]
    if reverse:
        row_masks = [(sub <= j) if j < SUBLANES - 1 else None for j in range(SUBLANES)]
    else:
        row_masks = [(sub >= j) if j > 0 else None for j in range(SUBLANES)]

    lbr = lb_ref[...]
    e = jnp.exp(lbr - jnp.max(lbr, axis=0, keepdims=True))
    lb_all = jnp.sum(e[1:layer + 1], axis=0, keepdims=True) / jnp.sum(e, axis=0, keepdims=True) \
        if layer > 0 else jnp.zeros((1, width), F32)

    for c in (range(nch - 1, -1, -1) if reverse else range(nch)):
        base = SCAN_CHUNK * c
        rs = pl.ds(base, SCAN_CHUNK)
        f = lb_all + (1.0 - lb_all) * jax.nn.sigmoid(x_ref[rs, :])
        g = jnp.log(f) * LOG2E
        ghi = g.astype(BF16)
        glo = (g - ghi.astype(F32)).astype(BF16)
        b2 = jnp.dot(tmat, jnp.concatenate([ghi, glo], axis=1), preferred_element_type=F32)
        k_scr[rs, :] = 1.0 - f
        b_scr[rs, :] = b2[:, :width] + b2[:, width:]
        for hh in range(SCAN_HEADS):
            sl = slice(LANES * hh, LANES * (hh + 1))
            o, st = _chunk(q_ref, k_scr, b_scr, v_ref, m_ref, base, sl, st_ref[hh], reverse,
                           row_masks, lane_masks)
            if has_gate:
                o = _rms(o + of_ref[rs, sl]) * gn_ref[...]
                o_ref[rs, sl] = (o * _silu(og_ref[rs, sl].astype(F32))).astype(BF16)
            else:
                o_ref[rs, sl] = o
            st_ref[hh] = st

    @pl.when(i == pl.num_programs(2) - 1)
    def _():
        for hh in range(SCAN_HEADS):
            sfin_ref[hh] = st_ref[hh].T


def scan_masks(reverse):
    t = jnp.arange(SCAN_CHUNK, dtype=jnp.int32)[:, None]
    s = jnp.arange(SCAN_CHUNK, dtype=jnp.int32)[None, :]
    diff = t ^ s
    strict = (s > t) if reverse else (s < t)
    masks = [(diff >= h) & (diff < 2 * h) & strict for h in SCAN_LEVELS] + [diff < SUBLANES]
    return jnp.stack(masks).astype(F32)


def scan(p, lb_dir, state, layer, direction, gate=None, state_buf=None):
    bsz, n, _ = p.shape
    reverse = direction == 1
    nblk = n // SCAN_ROWS
    wblk = LANES * SCAN_HEADS
    has_s0 = state is not None
    has_gate = gate is not None
    has_sbuf = state_buf is not None

    def rowblk(i):
        return nblk - 1 - i if reverse else i

    def col(c0):
        return lambda b, h, i: (b, rowblk(i), c0 // wblk + h)

    in_specs = [
        pl.BlockSpec((None, SCAN_ROWS, wblk), col(COL_Q)),
        pl.BlockSpec((None, SCAN_ROWS, wblk), col(COL_FB if reverse else COL_FF)),
        pl.BlockSpec((None, SCAN_ROWS, wblk), col(COL_IV)),
        pl.BlockSpec((None, DEPTH, wblk), lambda b, h, i: (direction, 0, h)),
        pl.BlockSpec((len(SCAN_LEVELS) + 1, SCAN_CHUNK, SCAN_CHUNK), lambda b, h, i: (0, 0, 0)),
    ]
    args = [p, p, p, lb_dir, scan_masks(reverse)]
    if has_s0:
        in_specs.append(pl.BlockSpec((None, None, None, SCAN_HEADS, DK, DV),
                                     lambda b, h, i: (b, layer, direction, h, 0, 0)))
        args.append(state)
    if has_gate:
        o_other, p_gate, g_norm = gate
        in_specs += [
            pl.BlockSpec((None, SCAN_ROWS, wblk), lambda b, h, i: (b, rowblk(i), h)),
            pl.BlockSpec((None, SCAN_ROWS, wblk), lambda b, h, i: (b, rowblk(i), GCOL_OG // wblk + h)),
            pl.BlockSpec((None, 1, DV), lambda b, h, i: (layer, 0, 0)),
        ]
        args += [o_other, p_gate, g_norm]
    aliases = {}
    if has_sbuf:
        aliases = {len(args): 1}
        in_specs.append(pl.BlockSpec(memory_space=pl.ANY))
        args.append(state_buf)
        s_spec = pl.BlockSpec((None, None, None, SCAN_HEADS, DK, DV),
                              lambda b, h, i: (b, layer, direction, h, 0, 0))
        s_shape = jax.ShapeDtypeStruct(state_buf.shape, F32)
    else:
        s_spec = pl.BlockSpec((None, SCAN_HEADS, DK, DV), lambda b, h, i: (b, h, 0, 0))
        s_shape = jax.ShapeDtypeStruct((bsz, H_A, DK, DV), F32)
    return pl.pallas_call(
        functools.partial(_scan_kernel, layer=layer, reverse=reverse, has_s0=has_s0,
                          has_gate=has_gate, has_sbuf=has_sbuf),
        grid=(bsz, H_A // SCAN_HEADS, nblk),
        in_specs=in_specs,
        out_specs=[
            pl.BlockSpec((None, SCAN_ROWS, wblk), lambda b, h, i: (b, rowblk(i), h)),
            s_spec,
        ],
        out_shape=[
            jax.ShapeDtypeStruct((bsz, n, W_A), BF16 if has_gate else F32),
            s_shape,
        ],
        input_output_aliases=aliases,
        scratch_shapes=[pltpu.VMEM((SCAN_HEADS, DV, DK), F32),
                        pltpu.VMEM((SCAN_ROWS, wblk), F32),
                        pltpu.VMEM((SCAN_ROWS, wblk), F32)],
        compiler_params=_params(("parallel", "parallel", "arbitrary")),
        name="scan_bwd" if reverse else "scan_fwd",
    )(*args)


def _chan_dft_kernel(u_ref, cs_ref, a_ref, b_ref):
    cs = cs_ref[...]
    for g in range(N_FG):
        sl = slice(FG * g, FG * (g + 1))
        r = jnp.dot(u_ref[:, sl], cs, preferred_element_type=F32)
        a_ref[:, sl] = r[:, :FG].astype(BF16)
        b_ref[:, sl] = r[:, FG:].astype(BF16)


def chan_dft(p, cs_f):
    bsz, n, _ = p.shape
    tm = min(TM_CDFT, n)
    out = jax.ShapeDtypeStruct((bsz, n, W_B), BF16)
    spec = pl.BlockSpec((None, tm, W_B), lambda b, i: (b, i, 0))
    return pl.pallas_call(
        _chan_dft_kernel,
        grid=(bsz, n // tm),
        in_specs=[
            pl.BlockSpec((None, tm, W_B), lambda b, i: (b, i, GCOL_U // W_B)),
            pl.BlockSpec((FG, 2 * FG), lambda b, i: (0, 0)),
        ],
        out_specs=[spec, spec],
        out_shape=[out, out],
        compiler_params=_params(("parallel", "parallel")),
        name="chan_dft",
    )(p, cs_f)


def _seq_dft_kernel(c_ref, s_ref, a_ref, b_ref, z_ref, *, scale):
    acc = jnp.dot(c_ref[...], a_ref[...], preferred_element_type=F32)
    acc = acc - jnp.dot(s_ref[...], b_ref[...], preferred_element_type=F32)
    z_ref[...] = (acc * scale).astype(BF16)


def seq_dft(a, b, cos_n, sin_n):
    bsz, n, _ = a.shape
    tk = min(TK_SDFT, n)
    full = pl.BlockSpec((None, n, W_B), lambda bi, i: (bi, 0, 0))
    slab = pl.BlockSpec((tk, n), lambda bi, i: (i, 0))
    return pl.pallas_call(
        functools.partial(_seq_dft_kernel, scale=1.0 / math.sqrt(n * FG)),
        grid=(bsz, n // tk),
        in_specs=[slab, slab, full, full],
        out_specs=pl.BlockSpec((None, tk, W_B), lambda bi, i: (bi, i, 0)),
        out_shape=jax.ShapeDtypeStruct((bsz, n, W_B), BF16),
        compiler_params=_params(("parallel", "parallel")),
        name="seq_dft",
    )(cos_n, sin_n, a, b)


def dft_tables(n):
    idx = jnp.arange(n, dtype=jnp.int32)
    ang = ((idx[:, None] * idx[None, :]) % n).astype(F32) * (2.0 * math.pi / n)
    return jnp.cos(ang).astype(BF16), jnp.sin(ang).astype(BF16)


DFT_RADIX = 64
DFT_GROUP = 8
DFT_STEP = 16


def _kron_rows(mat, inner):
    r, c = mat.shape
    eye = jnp.eye(inner, dtype=mat.dtype)
    return (mat[:, None, :, None] * eye[None, :, None, :]).reshape(r * inner, c * inner)


def dft2_constants(n):
    m = DFT_RADIX
    idx = jnp.arange(m, dtype=jnp.int32)
    ang = ((idx[:, None] * idx[None, :]) % m).astype(F32) * (2.0 * math.pi / m)
    cm, sm = jnp.cos(ang), jnp.sin(ang)
    ck, sk = _kron_rows(cm, DFT_GROUP), _kron_rows(sm, DFT_GROUP)
    l1 = jnp.concatenate([jnp.concatenate([ck, -sk], axis=1),
                          jnp.concatenate([sk, ck], axis=1)], axis=0).astype(BF16)
    eye = jnp.eye(DFT_GROUP, dtype=F32)
    c2 = (cm[:, None, None, :] * eye[None, :, :, None]).reshape(m * DFT_GROUP, DFT_GROUP * m)
    s2 = (sm[:, None, None, :] * eye[None, :, :, None]).reshape(m * DFT_GROUP, DFT_GROUP * m)
    l2 = jnp.concatenate([c2, -s2], axis=1).astype(BF16)
    k1 = jnp.arange(m, dtype=jnp.int32)[None, :, None]
    n2 = (jnp.arange(m // DFT_GROUP, dtype=jnp.int32)[:, None, None] * DFT_GROUP
          + jnp.arange(DFT_GROUP, dtype=jnp.int32)[None, None, :])
    tang = ((k1 * n2) % n).astype(F32) * (2.0 * math.pi / n)
    tang = jnp.broadcast_to(tang.reshape(m // DFT_GROUP, m * DFT_GROUP, 1),
                            (m // DFT_GROUP, m * DFT_GROUP, LANES))
    return l1, l2, jnp.cos(tang), jnp.sin(tang)


def _dft_stage1_kernel(u_ref, cs_ref, l1_ref, tc_ref, ts_ref, yr_ref, yi_ref):
    m = DFT_RADIX
    rows = m * DFT_GROUP
    cs = cs_ref[...]
    l1 = l1_ref[...]
    yr_parts, yi_parts = [], []
    u_all = u_ref[...].astype(F32)
    for hf in range(DFT_STEP // DFT_GROUP):
        u = u_all[:, DFT_GROUP * hf:DFT_GROUP * (hf + 1), :].reshape(rows, W_B)
        a_parts, b_parts = [], []
        for g in range(N_FG):
            r = jnp.dot(u[:, FG * g:FG * (g + 1)].astype(BF16), cs, preferred_element_type=F32)
            a_parts.append(r[:, :FG].astype(BF16))
            b_parts.append(r[:, FG:].astype(BF16))
        x = jnp.concatenate([jnp.concatenate(a_parts, axis=1), jnp.concatenate(b_parts, axis=1)], axis=0)
        y = jnp.dot(l1, x, preferred_element_type=F32)
        yr, yi = y[:rows], y[rows:]
        tc, ts = tc_ref[hf], ts_ref[hf]
        pr, pi = [], []
        for c in range(W_B // LANES):
            sl = slice(LANES * c, LANES * (c + 1))
            pr.append(yr[:, sl] * tc - yi[:, sl] * ts)
            pi.append(yr[:, sl] * ts + yi[:, sl] * tc)
        yr_parts.append(jnp.concatenate(pr, axis=1).reshape(m, DFT_GROUP, W_B))
        yi_parts.append(jnp.concatenate(pi, axis=1).reshape(m, DFT_GROUP, W_B))
    yr_ref[...] = jnp.concatenate(yr_parts, axis=1).astype(BF16)
    yi_ref[...] = jnp.concatenate(yi_parts, axis=1).astype(BF16)


def _dft_stage2_kernel(yr_ref, yi_ref, l2_ref, z_ref, *, scale):
    m = DFT_RADIX
    rows = m * DFT_GROUP
    l2 = l2_ref[...]
    parts = []
    for hf in range(DFT_STEP // DFT_GROUP):
        sl = slice(DFT_GROUP * hf, DFT_GROUP * (hf + 1))
        x = jnp.concatenate([yr_ref[sl].reshape(rows, W_B), yi_ref[sl].reshape(rows, W_B)], axis=0)
        z = jnp.dot(l2, x, preferred_element_type=F32) * scale
        parts.append(z.reshape(m, DFT_GROUP, W_B))
    z_ref[...] = jnp.concatenate(parts, axis=1).astype(BF16)


def fourier_two_stage(p, cs_f, consts):
    bsz, n, _ = p.shape
    m = DFT_RADIX
    l1, l2, tw_c, tw_s = consts
    p4 = p.reshape(bsz, m, m, N_GATE)
    nstep = m // DFT_STEP
    per_step = DFT_STEP // DFT_GROUP
    ybuf = jax.ShapeDtypeStruct((bsz, m, m, W_B), BF16)
    yspec = pl.BlockSpec((None, m, DFT_STEP, W_B), lambda b, j: (b, 0, j, 0))
    twspec = pl.BlockSpec((per_step, m * DFT_GROUP, LANES), lambda b, j: (j, 0, 0))
    yr, yi = pl.pallas_call(
        _dft_stage1_kernel,
        grid=(bsz, nstep),
        in_specs=[
            pl.BlockSpec((None, m, DFT_STEP, W_B), lambda b, j: (b, 0, j, GCOL_U // W_B)),
            pl.BlockSpec((FG, 2 * FG), lambda b, j: (0, 0)),
            pl.BlockSpec(l1.shape, lambda b, j: (0, 0)),
            twspec, twspec,
        ],
        out_specs=[yspec, yspec],
        out_shape=[ybuf, ybuf],
        compiler_params=_params(("parallel", "parallel")),
        name="dft_stage1",
    )(p4, cs_f, l1, tw_c, tw_s)
    xspec = pl.BlockSpec((None, DFT_STEP, m, W_B), lambda b, i: (b, i, 0, 0))
    z = pl.pallas_call(
        functools.partial(_dft_stage2_kernel, scale=1.0 / math.sqrt(n * FG)),
        grid=(bsz, nstep),
        in_specs=[xspec, xspec, pl.BlockSpec(l2.shape, lambda b, i: (0, 0))],
        out_specs=pl.BlockSpec((None, m, DFT_STEP, W_B), lambda b, i: (b, 0, i, 0)),
        out_shape=jax.ShapeDtypeStruct((bsz, m, m, W_B), BF16),
        compiler_params=_params(("parallel", "parallel")),
        name="dft_stage2",
    )(yr, yi, l2)
    return z.reshape(bsz, n, W_B)


def _mix_out_kernel(o_ref, z_ref, ga_ref, gb_ref, x_ref, mod_ref, wa_ref, wb_ref, wo_ref, out_ref):
    ya = jnp.dot(o_ref[...], wa_ref[...], preferred_element_type=F32)
    yb = jnp.dot(z_ref[...], wb_ref[...], preferred_element_type=F32)
    merged = (jax.nn.sigmoid(ga_ref[...].astype(F32)) * ya
              + jax.nn.sigmoid(gb_ref[...].astype(F32)) * yb)
    m = jnp.dot(merged.astype(BF16), wo_ref[...], preferred_element_type=F32)
    out_ref[...] = x_ref[...] + mod_ref[2:3, :] * m


def mix_out(o, p_gate, z, x, mod, w_a_bf, w_b_bf, w_out_bf, layer, row0):
    bsz, n, _ = x.shape
    tm = min(TM_MIX, n)

    def tok(width, cblk=0):
        return pl.BlockSpec((None, tm, width), lambda b, i: (b, i, cblk))

    def whole(rows, cols):
        return pl.BlockSpec((None, rows, cols), lambda b, i: (layer, 0, 0),
                            pipeline_mode=pl.Buffered(1))

    in_specs = [tok(W_A), tok(W_B), tok(D_MODEL, GCOL_GA // D_MODEL), tok(D_MODEL, GCOL_GB // D_MODEL),
                tok(D_MODEL)]
    args = [o, z, p_gate, p_gate, x]
    in_specs += [
        pl.BlockSpec((None, None, 6, D_MODEL), lambda b, i: (layer, row0 + b, 0, 0)),
        whole(W_A, D_MODEL), whole(W_B, D_MODEL), whole(D_MODEL, D_MODEL),
    ]
    args += [mod, w_a_bf, w_b_bf, w_out_bf]
    return pl.pallas_call(
        _mix_out_kernel,
        grid=(bsz, n // tm),
        in_specs=in_specs,
        out_specs=tok(D_MODEL),
        out_shape=jax.ShapeDtypeStruct((bsz, n, D_MODEL), F32),
        compiler_params=_params(("parallel", "parallel")),
        name="mix_out",
    )(*args)


def _ffn_kernel(*refs, has_final):
    if has_final:
        x_ref, mod_ref, n_ref, w1a_ref, w1g_ref, w2_ref, nf_ref, out_ref, h_ref = refs
    else:
        x_ref, mod_ref, n_ref, w1a_ref, w1g_ref, w2_ref, out_ref, h_ref = refs
    j = pl.program_id(2)
    rows = x_ref.shape[0]
    chunk = min(PROLOGUE_ROWS, rows)

    def partial_out(h):
        a = jnp.dot(h, w1a_ref[...], preferred_element_type=F32)
        gt = jnp.dot(h, w1g_ref[...], preferred_element_type=F32)
        return jnp.dot((_silu(a) * gt).astype(BF16), w2_ref[...], preferred_element_type=F32)

    @pl.when(j == 0)
    def _():
        for r0 in range(0, rows, chunk):
            rs = pl.ds(r0, chunk)
            h = _rms(x_ref[rs, :]) * n_ref[...]
            h = (h * (1.0 + mod_ref[4:5, :]) + mod_ref[3:4, :]).astype(BF16)
            h_ref[rs, :] = h
            out_ref[rs, :] = partial_out(h)

    @pl.when(j > 0)
    def _():
        out_ref[...] += partial_out(h_ref[...])

    @pl.when(j == pl.num_programs(2) - 1)
    def _():
        y = x_ref[...] + mod_ref[5:6, :] * out_ref[...]
        if has_final:
            y = _rms(y) * nf_ref[...]
        out_ref[...] = y


def ffn(x, mod, norm_w, w_ff_in_bf, w_ff_out_bf, norm_final, layer, row0):
    bsz, n, _ = x.shape
    tm = min(TM_FFN, n)
    nj = D_FF // TN_FFN
    has_final = norm_final is not None
    in_specs = [
        pl.BlockSpec((None, tm, D_MODEL), lambda b, i, j: (b, i, 0)),
        pl.BlockSpec((None, None, 6, D_MODEL), lambda b, i, j: (layer, row0 + b, 0, 0)),
        pl.BlockSpec((None, 1, D_MODEL), lambda b, i, j: (layer, 0, 0)),
        pl.BlockSpec((None, D_MODEL, TN_FFN), lambda b, i, j: (layer, 0, j)),
        pl.BlockSpec((None, D_MODEL, TN_FFN), lambda b, i, j: (layer, 0, nj + j)),
        pl.BlockSpec((None, TN_FFN, D_MODEL), lambda b, i, j: (layer, j, 0)),
    ]
    args = [x, mod, norm_w, w_ff_in_bf, w_ff_in_bf, w_ff_out_bf]
    if has_final:
        in_specs.append(pl.BlockSpec((1, D_MODEL), lambda b, i, j: (0, 0)))
        args.append(norm_final)
    return pl.pallas_call(
        functools.partial(_ffn_kernel, has_final=has_final),
        grid=(bsz, n // tm, nj),
        in_specs=in_specs,
        out_specs=pl.BlockSpec((None, tm, D_MODEL), lambda b, i, j: (b, i, 0)),
        out_shape=jax.ShapeDtypeStruct((bsz, n, D_MODEL), F32),
        scratch_shapes=[pltpu.VMEM((tm, D_MODEL), BF16)],
        compiler_params=_params(("parallel", "parallel", "arbitrary")),
        name="ffn",
    )(*args)


def grid_pos_embed(n):
    rows = n // GRID_W
    quarter = D_MODEL // 4
    omega = 1.0 / (POS_BASE ** (jnp.arange(quarter, dtype=F32) / quarter))

    def enc(count):
        a = jnp.arange(count, dtype=F32)[:, None] * omega[None, :]
        return jnp.concatenate([jnp.sin(a), jnp.cos(a)], axis=-1)

    er = jnp.broadcast_to(enc(rows)[:, None, :], (rows, GRID_W, 2 * quarter))
    ec = jnp.broadcast_to(enc(GRID_W)[None, :, :], (rows, GRID_W, 2 * quarter))
    return jnp.concatenate([er, ec], axis=-1).reshape(n, D_MODEL)


def _add_pos_kernel(x_ref, pe_ref, out_ref):
    out_ref[...] = x_ref[...] + pe_ref[...]


def add_pos(x, pe):
    bsz, n, _ = x.shape
    tm = min(TM_IN, n)
    return pl.pallas_call(
        _add_pos_kernel,
        grid=(n // tm, bsz),
        in_specs=[pl.BlockSpec((None, tm, D_MODEL), lambda i, b: (b, i, 0)),
                  pl.BlockSpec((tm, D_MODEL), lambda i, b: (i, 0))],
        out_specs=pl.BlockSpec((None, tm, D_MODEL), lambda i, b: (b, i, 0)),
        out_shape=jax.ShapeDtypeStruct(x.shape, x.dtype),
        compiler_params=_params(("parallel", "parallel")),
        name="add_pos",
    )(x, pe)


def kernel(x_prompt, x_sample, state_hgrn, c, c_ctx, w_mod, b_mod, norm_mix, norm_ffn, w_in, lb_raw,
           g_norm, w_a, w_b, w_out, w_ff_in, w_ff_out, norm_final):
    bc, nc, _ = x_prompt.shape
    bl, nl, _ = x_sample.shape
    assert 1 + bl <= MOD_ROWS

    cvec = jnp.zeros((MOD_ROWS, D_MODEL), F32).at[0].set(c_ctx).at[1:1 + bl].set(c)
    mod = adaln_table(cvec, w_mod, b_mod)

    w_in_bf, w_a_bf, w_b_bf, w_out_bf = (w.astype(BF16) for w in (w_in, w_a, w_b, w_out))
    w_ff_in_bf, w_ff_out_bf = w_ff_in.astype(BF16), w_ff_out.astype(BF16)
    norm_mix3 = norm_mix.reshape(DEPTH, 1, D_MODEL)
    norm_ffn3 = norm_ffn.reshape(DEPTH, 1, D_MODEL)
    g_norm3 = g_norm.reshape(DEPTH, 1, DV)
    norm_final2 = norm_final.reshape(1, D_MODEL)
    lb_dir = jnp.transpose(lb_raw, (1, 0, 2))

    fidx = jnp.arange(FG, dtype=jnp.int32)
    fang = ((fidx[:, None] * fidx[None, :]) % FG).astype(F32) * (2.0 * math.pi / FG)
    cs_f = jnp.concatenate([jnp.cos(fang), jnp.sin(fang)], axis=1).astype(BF16)
    tables = {n: dft2_constants(n) if n == DFT_RADIX ** 2 else dft_tables(n) for n in {nc, nl}}

    def layer(x, seq_shape, state, sbuf, l, row0):
        bt, nt, _ = x.shape
        bs, ns = seq_shape
        p_rec, p_gate = proj_in(x, mod, norm_mix3, w_in_bf, l, row0)
        rec_s = p_rec.reshape(bs, ns, N_REC)
        gate_s = p_gate.reshape(bs, ns, N_GATE)
        o_f, s_f = scan(rec_s, lb_dir, state, l, 0, state_buf=sbuf)
        if sbuf is not None:
            sbuf = s_f
        o, s_b = scan(rec_s, lb_dir, state, l, 1, gate=(o_f, gate_s, g_norm3), state_buf=sbuf)
        if sbuf is not None:
            sbuf = s_b
        if ns == DFT_RADIX ** 2:
            z = fourier_two_stage(gate_s, cs_f, tables[ns])
        else:
            fa, fb = chan_dft(gate_s, cs_f)
            z = seq_dft(fa, fb, *tables[ns])
        x = mix_out(o.reshape(bt, nt, W_A), p_gate, z.reshape(bt, nt, W_B),
                    x, mod, w_a_bf, w_b_bf, w_out_bf, l, row0)
        x = ffn(x, mod, norm_ffn3, w_ff_in_bf, w_ff_out_bf,
                norm_final2 if l == DEPTH - 1 else None, l, row0)
        return x, sbuf

    xc = x_prompt.reshape(1, bc * nc, D_MODEL)
    xs = add_pos(x_sample, grid_pos_embed(nl))
    state_new = jnp.zeros((bc, DEPTH, 2, H_A, DK, DV), F32)
    for l in range(DEPTH):
        xc, state_new = layer(xc, (bc, nc), None, state_new, l, 0)
        xs, _ = layer(xs, (bl, nl), state_hgrn, None, l, 1)
    return xc.reshape(bc, nc, D_MODEL), xs, state_new.astype(x_prompt.dtype)
```

```python
import functools
import math

import jax
import jax.numpy as jnp
from jax import lax
from jax.experimental import pallas as pl
from jax.experimental.pallas import tpu as pltpu

D_MODEL = 2048
DEPTH = 2
H_A = 8
DK = 128
DV = 128
W_A = H_A * DK
N_FG = 4
FG = 256
W_B = N_FG * FG
D_FF = 5632
N_IN = 5 * W_A + W_B + 2 * D_MODEL
EPS = 1e-6
GRID_W = 64
POS_BASE = 10000.0

N_REC = 4 * W_A
N_GATE = N_IN - N_REC
COL_Q = 0
COL_FF = W_A
COL_FB = 2 * W_A
COL_IV = 3 * W_A
GCOL_OG = 0
GCOL_U = W_A
GCOL_GA = W_A + W_B
GCOL_GB = W_A + W_B + D_MODEL

MOD_ROWS = 8
LANES = 128
SUBLANES = 8
VMEM_LIMIT = 56 * 1024 * 1024

TM_IN, TN_IN = 1024, 1024
PROLOGUE_ROWS = 256
TM_MIX = 256
TM_FFN, TN_FFN = 1024, 256
FOURIER_BATCH = 4
SCAN_ROWS = 256
SCAN_CHUNK = 128
SCAN_LEVELS = (64, 32, 16, 8)
SCAN_HEADS = 8
LOG2E = 1.4426950408889634

F32 = jnp.float32
BF16 = jnp.bfloat16


def _params(sem):
    return pltpu.CompilerParams(dimension_semantics=sem, vmem_limit_bytes=VMEM_LIMIT)


def _silu(x):
    return x * jax.nn.sigmoid(x)


def _rms(x):
    return x * lax.rsqrt(jnp.mean(x * x, axis=-1, keepdims=True) + EPS)


def _adaln_kernel(c_ref, w_ref, b_ref, out_ref):
    s = _silu(c_ref[...]).astype(BF16)
    out_ref[...] = jnp.dot(s, w_ref[...].astype(BF16), preferred_element_type=F32) + b_ref[...]


def adaln_table(cvec, w_mod, b_mod):
    tn = 1024
    out = pl.pallas_call(
        _adaln_kernel,
        grid=(DEPTH, 6 * D_MODEL // tn),
        in_specs=[
            pl.BlockSpec((MOD_ROWS, D_MODEL), lambda l, j: (0, 0)),
            pl.BlockSpec((None, D_MODEL, tn), lambda l, j: (l, 0, j)),
            pl.BlockSpec((None, 1, tn), lambda l, j: (l, 0, j)),
        ],
        out_specs=pl.BlockSpec((None, MOD_ROWS, tn), lambda l, j: (l, 0, j)),
        out_shape=jax.ShapeDtypeStruct((DEPTH, MOD_ROWS, 6 * D_MODEL), F32),
        compiler_params=_params(("parallel", "parallel")),
        name="adaln",
    )(cvec, w_mod, b_mod.reshape(DEPTH, 1, 6 * D_MODEL))
    return out.reshape(DEPTH, MOD_ROWS, 6, D_MODEL)


def _proj_in_kernel(x_ref, mod_ref, n_ref, w_ref, rec_ref, gate_ref, h_ref):
    j = pl.program_id(2)
    n_rec = N_REC // w_ref.shape[1]
    rows = x_ref.shape[0]
    chunk = min(PROLOGUE_ROWS, rows)

    @pl.when(j == 0)
    def _():
        for r0 in range(0, rows, chunk):
            rs = pl.ds(r0, chunk)
            h = _rms(x_ref[rs, :]) * n_ref[...]
            h = (h * (1.0 + mod_ref[1:2, :]) + mod_ref[0:1, :]).astype(BF16)
            h_ref[rs, :] = h
            rec_ref[rs, :] = jnp.dot(h, w_ref[...], preferred_element_type=F32)

    @pl.when((j > 0) & (j < n_rec))
    def _():
        rec_ref[...] = jnp.dot(h_ref[...], w_ref[...], preferred_element_type=F32)

    @pl.when(j >= n_rec)
    def _():
        gate_ref[...] = jnp.dot(h_ref[...], w_ref[...], preferred_element_type=F32).astype(BF16)


def proj_in(x, mod, norm_w, w_in_bf, layer, row0):
    bsz, n, _ = x.shape
    tm = min(TM_IN, n)
    tn = TN_IN
    n_rec = N_REC // tn
    return pl.pallas_call(
        _proj_in_kernel,
        grid=(bsz, n // tm, N_IN // tn),
        in_specs=[
            pl.BlockSpec((None, tm, D_MODEL), lambda b, i, j: (b, i, 0)),
            pl.BlockSpec((None, None, 6, D_MODEL), lambda b, i, j: (layer, row0 + b, 0, 0)),
            pl.BlockSpec((None, 1, D_MODEL), lambda b, i, j: (layer, 0, 0)),
            pl.BlockSpec((None, D_MODEL, tn), lambda b, i, j: (layer, 0, j)),
        ],
        out_specs=[
            pl.BlockSpec((None, tm, tn), lambda b, i, j: (b, i, jnp.minimum(j, n_rec - 1))),
            pl.BlockSpec((None, tm, tn), lambda b, i, j: (b, i, jnp.maximum(j - n_rec, 0))),
        ],
        out_shape=[jax.ShapeDtypeStruct((bsz, n, N_REC), F32),
                   jax.ShapeDtypeStruct((bsz, n, N_GATE), BF16)],
        scratch_shapes=[pltpu.VMEM((tm, D_MODEL), BF16)],
        compiler_params=_params(("parallel", "parallel", "arbitrary")),
        name="proj_in",
    )(x, mod, norm_w, w_in_bf)


def _nt(a, b):
    return lax.dot_general(a, b, (((1,), (1,)), ((), ())), preferred_element_type=F32)


def _tn(a, b):
    return lax.dot_general(a, b, (((0,), (0,)), ((), ())), preferred_element_type=F32)


def _chunk(q_ref, k_ref, b_ref, v_ref, m_ref, base, sl, st, reverse, row_masks, lane_masks):
    def rows(ref, r0, n):
        return ref[pl.ds(base + r0, n), sl]

    def row(ref, r):
        return ref[pl.ds(base + r, 1), sl]

    q = rows(q_ref, 0, SCAN_CHUNK)
    k = rows(k_ref, 0, SCAN_CHUNK)
    b = rows(b_ref, 0, SCAN_CHUNK)
    vb = rows(v_ref, 0, SCAN_CHUNK).astype(BF16)
    b_end = row(b_ref, 0 if reverse else SCAN_CHUNK - 1)

    o = _nt((q * jnp.exp2(b)).astype(BF16), st.astype(BF16))

    levels = []
    for h in SCAN_LEVELS:
        z = jnp.zeros((h, LANES), F32)
        qp, kp = [], []
        for r0 in range(0, SCAN_CHUNK, 2 * h):
            if reverse:
                beta = row(b_ref, r0 + h)
                qp += [rows(q_ref, r0, h) * jnp.exp2(rows(b_ref, r0, h) - beta), z]
                kp += [z, rows(k_ref, r0 + h, h) * jnp.exp2(beta - rows(b_ref, r0 + h, h))]
            else:
                beta = row(b_ref, r0 + h - 1)
                kp += [rows(k_ref, r0, h) * jnp.exp2(beta - rows(b_ref, r0, h)), z]
                qp += [z, rows(q_ref, r0 + h, h) * jnp.exp2(rows(b_ref, r0 + h, h) - beta)]
        levels.append(_nt(jnp.concatenate(qp, axis=0).astype(BF16),
                          jnp.concatenate(kp, axis=0).astype(BF16)))

    parts = []
    for r0 in range(0, SCAN_CHUNK, SUBLANES):
        qv = rows(q_ref, r0, SUBLANES)
        bv = rows(b_ref, r0, SUBLANES)
        near = jnp.zeros((SUBLANES, LANES), F32)
        for j in range(SUBLANES):
            shape = (SUBLANES, LANES)
            kj = jnp.broadcast_to(row(k_ref, r0 + j), shape)
            bj = jnp.broadcast_to(row(b_ref, r0 + j), shape)
            d = bv - bj
            if row_masks[j] is not None:
                d = jnp.where(row_masks[j], d, -1e30)
            w = jnp.sum(qv * kj * jnp.exp2(d), axis=1, keepdims=True)
            near = jnp.where(lane_masks[j], w, near)
        rs = slice(r0, r0 + SUBLANES)
        acc = near * m_ref[len(SCAN_LEVELS), rs, :]
        for li, h in enumerate(SCAN_LEVELS):
            if ((r0 & h) == 0) == reverse:
                acc = acc + levels[li][rs] * m_ref[li, rs, :]
        parts.append(acc)
    a = jnp.concatenate(parts, axis=0)
    o = o + jnp.dot(a.astype(BF16), vb, preferred_element_type=F32)

    khat = (k * jnp.exp2(b_end - b)).astype(BF16)
    st = st * jnp.exp2(b_end) + _tn(vb, khat)
    return o, st


def _scan_kernel(*refs, layer, reverse, has_s0, has_gate, has_sbuf):
    refs = list(refs)
    q_ref, x_ref, v_ref, lb_ref, m_ref = refs[:5]
    del refs[:5]
    s0_ref = refs.pop(0) if has_s0 else None
    of_ref, og_ref, gn_ref = (refs.pop(0), refs.pop(0), refs.pop(0)) if has_gate else (None, None, None)
    if has_sbuf:
        refs.pop(0)
    o_ref, sfin_ref, st_ref, k_scr, b_scr = refs
    i = pl.program_id(2)
    nch = q_ref.shape[0] // SCAN_CHUNK
    width = q_ref.shape[1]

    @pl.when(i == 0)
    def _():
        for hh in range(SCAN_HEADS):
            if has_s0:
                st_ref[hh] = s0_ref[hh].T
            else:
                st_ref[hh] = jnp.zeros((DV, DK), F32)

    t_idx = lax.broadcasted_iota(jnp.int32, (SCAN_CHUNK, SCAN_CHUNK), 0)
    s_idx = lax.broadcasted_iota(jnp.int32, (SCAN_CHUNK, SCAN_CHUNK), 1)
    incl = (s_idx >= t_idx) if reverse else (s_idx <= t_idx)
    tmat = jnp.where(incl, 1.0, 0.0).astype(BF16)
    sub = lax.broadcasted_iota(jnp.int32, (SUBLANES, LANES), 0)
    lane = lax.broadcasted_iota(jnp.int32, (SUBLANES, LANES), 1)
    lane_masks = [(lane & (SUBLANES - 1)) == j for j in range(SUBLANES)]
    if reverse:
        row_masks = [(sub <= j) if j < SUBLANES - 1 else None for j in range(SUBLANES)]
    else:
        row_masks = [(sub >= j) if j > 0 else None for j in range(SUBLANES)]

    lbr = lb_ref[...]
    e = jnp.exp(lbr - jnp.max(lbr, axis=0, keepdims=True))
    lb_all = jnp.sum(e[1:layer + 1], axis=0, keepdims=True) / jnp.sum(e, axis=0, keepdims=True) \
        if layer > 0 else jnp.zeros((1, width), F32)

    for c in (range(nch - 1, -1, -1) if reverse else range(nch)):
        base = SCAN_CHUNK * c
        rs = pl.ds(base, SCAN_CHUNK)
        f = lb_all + (1.0 - lb_all) * jax.nn.sigmoid(x_ref[rs, :])
        g = jnp.log(f) * LOG2E
        ghi = g.astype(BF16)
        glo = (g - ghi.astype(F32)).astype(BF16)
        b2 = jnp.dot(tmat, jnp.concatenate([ghi, glo], axis=1), preferred_element_type=F32)
        k_scr[rs, :] = 1.0 - f
        b_scr[rs, :] = b2[:, :width] + b2[:, width:]
        for hh in range(SCAN_HEADS):
            sl = slice(LANES * hh, LANES * (hh + 1))
            o, st = _chunk(q_ref, k_scr, b_scr, v_ref, m_ref, base, sl, st_ref[hh], reverse,
                           row_masks, lane_masks)
            if has_gate:
                o = _rms(o + of_ref[rs, sl]) * gn_ref[...]
                o_ref[rs, sl] = (o * _silu(og_ref[rs, sl].astype(F32))).astype(BF16)
            else:
                o_ref[rs, sl] = o
            st_ref[hh] = st

    @pl.when(i == pl.num_programs(2) - 1)
    def _():
        for hh in range(SCAN_HEADS):
            sfin_ref[hh] = st_ref[hh].T


def scan_masks(reverse):
    t = jnp.arange(SCAN_CHUNK, dtype=jnp.int32)[:, None]
    s = jnp.arange(SCAN_CHUNK, dtype=jnp.int32)[None, :]
    diff = t ^ s
    strict = (s > t) if reverse else (s < t)
    masks = [(diff >= h) & (diff < 2 * h) & strict for h in SCAN_LEVELS] + [diff < SUBLANES]
    return jnp.stack(masks).astype(F32)


def scan(p, lb_dir, state, layer, direction, gate=None, state_buf=None):
    bsz, n, _ = p.shape
    reverse = direction == 1
    nblk = n // SCAN_ROWS
    wblk = LANES * SCAN_HEADS
    has_s0 = state is not None
    has_gate = gate is not None
    has_sbuf = state_buf is not None

    def rowblk(i):
        return nblk - 1 - i if reverse else i

    def col(c0):
        return lambda b, h, i: (b, rowblk(i), c0 // wblk + h)

    in_specs = [
        pl.BlockSpec((None, SCAN_ROWS, wblk), col(COL_Q)),
        pl.BlockSpec((None, SCAN_ROWS, wblk), col(COL_FB if reverse else COL_FF)),
        pl.BlockSpec((None, SCAN_ROWS, wblk), col(COL_IV)),
        pl.BlockSpec((None, DEPTH, wblk), lambda b, h, i: (direction, 0, h)),
        pl.BlockSpec((len(SCAN_LEVELS) + 1, SCAN_CHUNK, SCAN_CHUNK), lambda b, h, i: (0, 0, 0)),
    ]
    args = [p, p, p, lb_dir, scan_masks(reverse)]
    if has_s0:
        in_specs.append(pl.BlockSpec((None, None, None, SCAN_HEADS, DK, DV),
                                     lambda b, h, i: (b, layer, direction, h, 0, 0)))
        args.append(state)
    if has_gate:
        o_other, p_gate, g_norm = gate
        in_specs += [
            pl.BlockSpec((None, SCAN_ROWS, wblk), lambda b, h, i: (b, rowblk(i), h)),
            pl.BlockSpec((None, SCAN_ROWS, wblk), lambda b, h, i: (b, rowblk(i), GCOL_OG // wblk + h)),
            pl.BlockSpec((None, 1, DV), lambda b, h, i: (layer, 0, 0)),
        ]
        args += [o_other, p_gate, g_norm]
    aliases = {}
    if has_sbuf:
        aliases = {len(args): 1}
        in_specs.append(pl.BlockSpec(memory_space=pl.ANY))
        args.append(state_buf)
        s_spec = pl.BlockSpec((None, None, None, SCAN_HEADS, DK, DV),
                              lambda b, h, i: (b, layer, direction, h, 0, 0))
        s_shape = jax.ShapeDtypeStruct(state_buf.shape, F32)
    else:
        s_spec = pl.BlockSpec((None, SCAN_HEADS, DK, DV), lambda b, h, i: (b, h, 0, 0))
        s_shape = jax.ShapeDtypeStruct((bsz, H_A, DK, DV), F32)
    return pl.pallas_call(
        functools.partial(_scan_kernel, layer=layer, reverse=reverse, has_s0=has_s0,
                          has_gate=has_gate, has_sbuf=has_sbuf),
        grid=(bsz, H_A // SCAN_HEADS, nblk),
        in_specs=in_specs,
        out_specs=[
            pl.BlockSpec((None, SCAN_ROWS, wblk), lambda b, h, i: (b, rowblk(i), h)),
            s_spec,
        ],
        out_shape=[
            jax.ShapeDtypeStruct((bsz, n, W_A), BF16 if has_gate else F32),
            s_shape,
        ],
        input_output_aliases=aliases,
        scratch_shapes=[pltpu.VMEM((SCAN_HEADS, DV, DK), F32),
                        pltpu.VMEM((SCAN_ROWS, wblk), F32),
                        pltpu.VMEM((SCAN_ROWS, wblk), F32)],
        compiler_params=_params(("parallel", "parallel", "arbitrary")),
        name="scan_bwd" if reverse else "scan_fwd",
    )(*args)


def _fourier_direct_kernel(u_ref, cs_ref, c_ref, s_ref, z_ref, *, scale):
    cs = cs_ref[...]
    for i in range(u_ref.shape[0]):
        a_parts, b_parts = [], []
        for g in range(N_FG):
            r = jnp.dot(u_ref[i, :, FG * g:FG * (g + 1)], cs, preferred_element_type=F32)
            a_parts.append(r[:, :FG].astype(BF16))
            b_parts.append(r[:, FG:].astype(BF16))
        acc = jnp.dot(c_ref[...], jnp.concatenate(a_parts, axis=1), preferred_element_type=F32)
        acc = acc - jnp.dot(s_ref[...], jnp.concatenate(b_parts, axis=1), preferred_element_type=F32)
        z_ref[i] = (acc * scale).astype(BF16)


def fourier_direct(p, cs_f, cos_n, sin_n):
    bsz, n, _ = p.shape
    bb = math.gcd(FOURIER_BATCH, bsz)
    return pl.pallas_call(
        functools.partial(_fourier_direct_kernel, scale=1.0 / math.sqrt(n * FG)),
        grid=(bsz // bb,),
        in_specs=[
            pl.BlockSpec((bb, n, W_B), lambda b: (b, 0, GCOL_U // W_B)),
            pl.BlockSpec((FG, 2 * FG), lambda b: (0, 0)),
            pl.BlockSpec((n, n), lambda b: (0, 0)),
            pl.BlockSpec((n, n), lambda b: (0, 0)),
        ],
        out_specs=pl.BlockSpec((bb, n, W_B), lambda b: (b, 0, 0)),
        out_shape=jax.ShapeDtypeStruct((bsz, n, W_B), BF16),
        compiler_params=_params(("parallel",)),
        name="fourier_direct",
    )(p, cs_f, cos_n, sin_n)


def dft_tables(n):
    idx = jnp.arange(n, dtype=jnp.int32)
    ang = ((idx[:, None] * idx[None, :]) % n).astype(F32) * (2.0 * math.pi / n)
    return jnp.cos(ang).astype(BF16), jnp.sin(ang).astype(BF16)


DFT_RADIX = 64
DFT_GROUP = 8
DFT_STEP = 16


def _kron_rows(mat, inner):
    r, c = mat.shape
    eye = jnp.eye(inner, dtype=mat.dtype)
    return (mat[:, None, :, None] * eye[None, :, None, :]).reshape(r * inner, c * inner)


def dft2_constants(n):
    m = DFT_RADIX
    idx = jnp.arange(m, dtype=jnp.int32)
    ang = ((idx[:, None] * idx[None, :]) % m).astype(F32) * (2.0 * math.pi / m)
    cm, sm = jnp.cos(ang), jnp.sin(ang)
    ck, sk = _kron_rows(cm, DFT_GROUP), _kron_rows(sm, DFT_GROUP)
    l1 = jnp.concatenate([jnp.concatenate([ck, -sk], axis=1),
                          jnp.concatenate([sk, ck], axis=1)], axis=0).astype(BF16)
    eye = jnp.eye(DFT_GROUP, dtype=F32)
    c2 = (cm[:, None, None, :] * eye[None, :, :, None]).reshape(m * DFT_GROUP, DFT_GROUP * m)
    s2 = (sm[:, None, None, :] * eye[None, :, :, None]).reshape(m * DFT_GROUP, DFT_GROUP * m)
    l2 = jnp.concatenate([c2, -s2], axis=1).astype(BF16)
    k1 = jnp.arange(m, dtype=jnp.int32)[None, :, None]
    n2 = (jnp.arange(m // DFT_GROUP, dtype=jnp.int32)[:, None, None] * DFT_GROUP
          + jnp.arange(DFT_GROUP, dtype=jnp.int32)[None, None, :])
    tang = ((k1 * n2) % n).astype(F32) * (2.0 * math.pi / n)
    tang = jnp.broadcast_to(tang.reshape(m // DFT_GROUP, m * DFT_GROUP, 1),
                            (m // DFT_GROUP, m * DFT_GROUP, LANES))
    return l1, l2, jnp.cos(tang), jnp.sin(tang)


def _dft_stage1_kernel(u_ref, cs_ref, l1_ref, tc_ref, ts_ref, yr_ref, yi_ref):
    m = DFT_RADIX
    rows = m * DFT_GROUP
    cs = cs_ref[...]
    l1 = l1_ref[...]
    yr_parts, yi_parts = [], []
    u_all = u_ref[...].astype(F32)
    for hf in range(DFT_STEP // DFT_GROUP):
        u = u_all[:, DFT_GROUP * hf:DFT_GROUP * (hf + 1), :].reshape(rows, W_B)
        a_parts, b_parts = [], []
        for g in range(N_FG):
            r = jnp.dot(u[:, FG * g:FG * (g + 1)].astype(BF16), cs, preferred_element_type=F32)
            a_parts.append(r[:, :FG].astype(BF16))
            b_parts.append(r[:, FG:].astype(BF16))
        x = jnp.concatenate([jnp.concatenate(a_parts, axis=1), jnp.concatenate(b_parts, axis=1)], axis=0)
        y = jnp.dot(l1, x, preferred_element_type=F32)
        yr, yi = y[:rows], y[rows:]
        tc, ts = tc_ref[hf], ts_ref[hf]
        pr, pi = [], []
        for c in range(W_B // LANES):
            sl = slice(LANES * c, LANES * (c + 1))
            pr.append(yr[:, sl] * tc - yi[:, sl] * ts)
            pi.append(yr[:, sl] * ts + yi[:, sl] * tc)
        yr_parts.append(jnp.concatenate(pr, axis=1).reshape(m, DFT_GROUP, W_B))
        yi_parts.append(jnp.concatenate(pi, axis=1).reshape(m, DFT_GROUP, W_B))
    yr_ref[...] = jnp.concatenate(yr_parts, axis=1).astype(BF16)
    yi_ref[...] = jnp.concatenate(yi_parts, axis=1).astype(BF16)


def _dft_stage2_kernel(yr_ref, yi_ref, l2_ref, z_ref, *, scale):
    m = DFT_RADIX
    rows = m * DFT_GROUP
    l2 = l2_ref[...]
    parts = []
    for hf in range(DFT_STEP // DFT_GROUP):
        sl = slice(DFT_GROUP * hf, DFT_GROUP * (hf + 1))
        x = jnp.concatenate([yr_ref[sl].reshape(rows, W_B), yi_ref[sl].reshape(rows, W_B)], axis=0)
        z = jnp.dot(l2, x, preferred_element_type=F32) * scale
        parts.append(z.reshape(m, DFT_GROUP, W_B))
    z_ref[...] = jnp.concatenate(parts, axis=1).astype(BF16)


def fourier_two_stage(p, cs_f, consts):
    bsz, n, _ = p.shape
    m = DFT_RADIX
    l1, l2, tw_c, tw_s = consts
    p4 = p.reshape(bsz, m, m, N_GATE)
    nstep = m // DFT_STEP
    per_step = DFT_STEP // DFT_GROUP
    ybuf = jax.ShapeDtypeStruct((bsz, m, m, W_B), BF16)
    yspec = pl.BlockSpec((None, m, DFT_STEP, W_B), lambda b, j: (b, 0, j, 0))
    twspec = pl.BlockSpec((per_step, m * DFT_GROUP, LANES), lambda b, j: (j, 0, 0))
    yr, yi = pl.pallas_call(
        _dft_stage1_kernel,
        grid=(bsz, nstep),
        in_specs=[
            pl.BlockSpec((None, m, DFT_STEP, W_B), lambda b, j: (b, 0, j, GCOL_U // W_B)),
            pl.BlockSpec((FG, 2 * FG), lambda b, j: (0, 0)),
            pl.BlockSpec(l1.shape, lambda b, j: (0, 0)),
            twspec, twspec,
        ],
        out_specs=[yspec, yspec],
        out_shape=[ybuf, ybuf],
        compiler_params=_params(("parallel", "parallel")),
        name="dft_stage1",
    )(p4, cs_f, l1, tw_c, tw_s)
    xspec = pl.BlockSpec((None, DFT_STEP, m, W_B), lambda b, i: (b, i, 0, 0))
    z = pl.pallas_call(
        functools.partial(_dft_stage2_kernel, scale=1.0 / math.sqrt(n * FG)),
        grid=(bsz, nstep),
        in_specs=[xspec, xspec, pl.BlockSpec(l2.shape, lambda b, i: (0, 0))],
        out_specs=pl.BlockSpec((None, m, DFT_STEP, W_B), lambda b, i: (b, 0, i, 0)),
        out_shape=jax.ShapeDtypeStruct((bsz, m, m, W_B), BF16),
        compiler_params=_params(("parallel", "parallel")),
        name="dft_stage2",
    )(yr, yi, l2)
    return z.reshape(bsz, n, W_B)


def _mix_out_kernel(o_ref, z_ref, ga_ref, gb_ref, x_ref, mod_ref, wa_ref, wb_ref, wo_ref, out_ref):
    ya = jnp.dot(o_ref[...], wa_ref[...], preferred_element_type=F32)
    yb = jnp.dot(z_ref[...], wb_ref[...], preferred_element_type=F32)
    merged = (jax.nn.sigmoid(ga_ref[...].astype(F32)) * ya
              + jax.nn.sigmoid(gb_ref[...].astype(F32)) * yb)
    m = jnp.dot(merged.astype(BF16), wo_ref[...], preferred_element_type=F32)
    out_ref[...] = x_ref[...] + mod_ref[2:3, :] * m


def mix_out(o, p_gate, z, x, mod, w_a_bf, w_b_bf, w_out_bf, layer, row0):
    bsz, n, _ = x.shape
    tm = min(TM_MIX, n)

    def tok(width, cblk=0):
        return pl.BlockSpec((None, tm, width), lambda b, i: (b, i, cblk))

    def whole(rows, cols):
        return pl.BlockSpec((None, rows, cols), lambda b, i: (layer, 0, 0),
                            pipeline_mode=pl.Buffered(1))

    in_specs = [tok(W_A), tok(W_B), tok(D_MODEL, GCOL_GA // D_MODEL), tok(D_MODEL, GCOL_GB // D_MODEL),
                tok(D_MODEL)]
    args = [o, z, p_gate, p_gate, x]
    in_specs += [
        pl.BlockSpec((None, None, 6, D_MODEL), lambda b, i: (layer, row0 + b, 0, 0)),
        whole(W_A, D_MODEL), whole(W_B, D_MODEL), whole(D_MODEL, D_MODEL),
    ]
    args += [mod, w_a_bf, w_b_bf, w_out_bf]
    return pl.pallas_call(
        _mix_out_kernel,
        grid=(bsz, n // tm),
        in_specs=in_specs,
        out_specs=tok(D_MODEL),
        out_shape=jax.ShapeDtypeStruct((bsz, n, D_MODEL), F32),
        compiler_params=_params(("parallel", "parallel")),
        name="mix_out",
    )(*args)


def _ffn_kernel(*refs, has_final):
    if has_final:
        x_ref, mod_ref, n_ref, w1a_ref, w1g_ref, w2_ref, nf_ref, out_ref, h_ref = refs
    else:
        x_ref, mod_ref, n_ref, w1a_ref, w1g_ref, w2_ref, out_ref, h_ref = refs
    j = pl.program_id(2)
    rows = x_ref.shape[0]
    chunk = min(PROLOGUE_ROWS, rows)

    def partial_out(h):
        a = jnp.dot(h, w1a_ref[...], preferred_element_type=F32)
        gt = jnp.dot(h, w1g_ref[...], preferred_element_type=F32)
        return jnp.dot((_silu(a) * gt).astype(BF16), w2_ref[...], preferred_element_type=F32)

    @pl.when(j == 0)
    def _():
        for r0 in range(0, rows, chunk):
            rs = pl.ds(r0, chunk)
            h = _rms(x_ref[rs, :]) * n_ref[...]
            h = (h * (1.0 + mod_ref[4:5, :]) + mod_ref[3:4, :]).astype(BF16)
            h_ref[rs, :] = h
            out_ref[rs, :] = partial_out(h)

    last = pl.num_programs(2) - 1

    @pl.when((j > 0) & (j < last))
    def _():
        out_ref[...] += partial_out(h_ref[...])

    @pl.when(j == last)
    def _():
        for r0 in range(0, rows, chunk):
            rs = pl.ds(r0, chunk)
            acc = out_ref[rs, :] + partial_out(h_ref[rs, :])
            y = x_ref[rs, :] + mod_ref[5:6, :] * acc
            if has_final:
                y = _rms(y) * nf_ref[...]
            out_ref[rs, :] = y


def ffn(x, mod, norm_w, w_ff_in_bf, w_ff_out_bf, norm_final, layer, row0):
    bsz, n, _ = x.shape
    tm = min(TM_FFN, n)
    nj = D_FF // TN_FFN
    has_final = norm_final is not None
    in_specs = [
        pl.BlockSpec((None, tm, D_MODEL), lambda b, i, j: (b, i, 0)),
        pl.BlockSpec((None, None, 6, D_MODEL), lambda b, i, j: (layer, row0 + b, 0, 0)),
        pl.BlockSpec((None, 1, D_MODEL), lambda b, i, j: (layer, 0, 0)),
        pl.BlockSpec((None, D_MODEL, TN_FFN), lambda b, i, j: (layer, 0, j)),
        pl.BlockSpec((None, D_MODEL, TN_FFN), lambda b, i, j: (layer, 0, nj + j)),
        pl.BlockSpec((None, TN_FFN, D_MODEL), lambda b, i, j: (layer, j, 0)),
    ]
    args = [x, mod, norm_w, w_ff_in_bf, w_ff_in_bf, w_ff_out_bf]
    if has_final:
        in_specs.append(pl.BlockSpec((1, D_MODEL), lambda b, i, j: (0, 0)))
        args.append(norm_final)
    return pl.pallas_call(
        functools.partial(_ffn_kernel, has_final=has_final),
        grid=(bsz, n // tm, nj),
        in_specs=in_specs,
        out_specs=pl.BlockSpec((None, tm, D_MODEL), lambda b, i, j: (b, i, 0)),
        out_shape=jax.ShapeDtypeStruct((bsz, n, D_MODEL), F32),
        scratch_shapes=[pltpu.VMEM((tm, D_MODEL), BF16)],
        compiler_params=_params(("parallel", "parallel", "arbitrary")),
        name="ffn",
    )(*args)


def grid_pos_embed(n):
    rows = n // GRID_W
    quarter = D_MODEL // 4
    omega = 1.0 / (POS_BASE ** (jnp.arange(quarter, dtype=F32) / quarter))

    def enc(count):
        a = jnp.arange(count, dtype=F32)[:, None] * omega[None, :]
        return jnp.concatenate([jnp.sin(a), jnp.cos(a)], axis=-1)

    er = jnp.broadcast_to(enc(rows)[:, None, :], (rows, GRID_W, 2 * quarter))
    ec = jnp.broadcast_to(enc(GRID_W)[None, :, :], (rows, GRID_W, 2 * quarter))
    return jnp.concatenate([er, ec], axis=-1).reshape(n, D_MODEL)


def _add_pos_kernel(x_ref, pe_ref, out_ref):
    out_ref[...] = x_ref[...] + pe_ref[...]


def add_pos(x, pe):
    bsz, n, _ = x.shape
    tm = min(TM_IN, n)
    return pl.pallas_call(
        _add_pos_kernel,
        grid=(n // tm, bsz),
        in_specs=[pl.BlockSpec((None, tm, D_MODEL), lambda i, b: (b, i, 0)),
                  pl.BlockSpec((tm, D_MODEL), lambda i, b: (i, 0))],
        out_specs=pl.BlockSpec((None, tm, D_MODEL), lambda i, b: (b, i, 0)),
        out_shape=jax.ShapeDtypeStruct(x.shape, x.dtype),
        compiler_params=_params(("parallel", "parallel")),
        name="add_pos",
    )(x, pe)


def kernel(x_prompt, x_sample, state_hgrn, c, c_ctx, w_mod, b_mod, norm_mix, norm_ffn, w_in, lb_raw,
           g_norm, w_a, w_b, w_out, w_ff_in, w_ff_out, norm_final):
    bc, nc, _ = x_prompt.shape
    bl, nl, _ = x_sample.shape
    assert 1 + bl <= MOD_ROWS

    cvec = jnp.zeros((MOD_ROWS, D_MODEL), F32).at[0].set(c_ctx).at[1:1 + bl].set(c)
    mod = adaln_table(cvec, w_mod, b_mod)

    w_in_bf, w_a_bf, w_b_bf, w_out_bf = (w.astype(BF16) for w in (w_in, w_a, w_b, w_out))
    w_ff_in_bf, w_ff_out_bf = w_ff_in.astype(BF16), w_ff_out.astype(BF16)
    norm_mix3 = norm_mix.reshape(DEPTH, 1, D_MODEL)
    norm_ffn3 = norm_ffn.reshape(DEPTH, 1, D_MODEL)
    g_norm3 = g_norm.reshape(DEPTH, 1, DV)
    norm_final2 = norm_final.reshape(1, D_MODEL)
    lb_dir = jnp.transpose(lb_raw, (1, 0, 2))

    fidx = jnp.arange(FG, dtype=jnp.int32)
    fang = ((fidx[:, None] * fidx[None, :]) % FG).astype(F32) * (2.0 * math.pi / FG)
    cs_f = jnp.concatenate([jnp.cos(fang), jnp.sin(fang)], axis=1).astype(BF16)
    tables = {n: dft2_constants(n) if n == DFT_RADIX ** 2 else dft_tables(n) for n in {nc, nl}}

    def layer(x, seq_shape, state, sbuf, l, row0):
        bt, nt, _ = x.shape
        bs, ns = seq_shape
        p_rec, p_gate = proj_in(x, mod, norm_mix3, w_in_bf, l, row0)
        rec_s = p_rec.reshape(bs, ns, N_REC)
        gate_s = p_gate.reshape(bs, ns, N_GATE)
        o_f, s_f = scan(rec_s, lb_dir, state, l, 0, state_buf=sbuf)
        if sbuf is not None:
            sbuf = s_f
        o, s_b = scan(rec_s, lb_dir, state, l, 1, gate=(o_f, gate_s, g_norm3), state_buf=sbuf)
        if sbuf is not None:
            sbuf = s_b
        if ns == DFT_RADIX ** 2:
            z = fourier_two_stage(gate_s, cs_f, tables[ns])
        else:
            z = fourier_direct(gate_s, cs_f, *tables[ns])
        x = mix_out(o.reshape(bt, nt, W_A), p_gate, z.reshape(bt, nt, W_B),
                    x, mod, w_a_bf, w_b_bf, w_out_bf, l, row0)
        x = ffn(x, mod, norm_ffn3, w_ff_in_bf, w_ff_out_bf,
                norm_final2 if l == DEPTH - 1 else None, l, row0)
        return x, sbuf

    xc = x_prompt.reshape(1, bc * nc, D_MODEL)
    xs = add_pos(x_sample, grid_pos_embed(nl))
    state_new = jnp.zeros((bc, DEPTH, 2, H_A, DK, DV), F32)
    for l in range(DEPTH):
        xc, state_new = layer(xc, (bc, nc), None, state_new, l, 0)
        xs, _ = layer(xs, (bl, nl), state_hgrn, None, l, 1)
    return xc.reshape(bc, nc, D_MODEL), xs, state_new.astype(x_prompt.dtype)
```

```python
import functools
import math

import jax
import jax.numpy as jnp
from jax import lax
from jax.experimental import pallas as pl
from jax.experimental.pallas import tpu as pltpu

D_MODEL = 2048
DEPTH = 2
H_A = 8
DK = 128
DV = 128
W_A = H_A * DK
N_FG = 4
FG = 256
W_B = N_FG * FG
D_FF = 5632
N_IN = 5 * W_A + W_B + 2 * D_MODEL
EPS = 1e-6
GRID_W = 64
POS_BASE = 10000.0

N_REC = 4 * W_A
N_GATE = N_IN - N_REC
COL_Q = 0
COL_FF = W_A
COL_FB = 2 * W_A
COL_IV = 3 * W_A
GCOL_OG = 0
GCOL_U = W_A
GCOL_GA = W_A + W_B
GCOL_GB = W_A + W_B + D_MODEL

MOD_ROWS = 8
LANES = 128
SUBLANES = 8
VMEM_LIMIT = 56 * 1024 * 1024

TM_IN, TN_IN = 1024, 1024
PROLOGUE_ROWS = 256
TM_MIX = 256
TM_FFN, TN_FFN = 1024, 512
FOURIER_BATCH = 4
SCAN_ROWS = 256
SCAN_CHUNK = 128
SCAN_LEVELS = (64, 32, 16, 8)
SCAN_HEADS = 8
LOG2E = 1.4426950408889634

F32 = jnp.float32
BF16 = jnp.bfloat16


def _params(sem):
    return pltpu.CompilerParams(dimension_semantics=sem, vmem_limit_bytes=VMEM_LIMIT)


def _silu(x):
    return x * jax.nn.sigmoid(x)


def _rms(x):
    return x * lax.rsqrt(jnp.mean(x * x, axis=-1, keepdims=True) + EPS)


def _adaln_kernel(c_ref, w_ref, b_ref, out_ref):
    s = _silu(c_ref[...]).astype(BF16)
    out_ref[...] = jnp.dot(s, w_ref[...].astype(BF16), preferred_element_type=F32) + b_ref[...]


def adaln_table(cvec, w_mod, b_mod):
    tn = 1024
    out = pl.pallas_call(
        _adaln_kernel,
        grid=(DEPTH, 6 * D_MODEL // tn),
        in_specs=[
            pl.BlockSpec((MOD_ROWS, D_MODEL), lambda l, j: (0, 0)),
            pl.BlockSpec((None, D_MODEL, tn), lambda l, j: (l, 0, j)),
            pl.BlockSpec((None, 1, tn), lambda l, j: (l, 0, j)),
        ],
        out_specs=pl.BlockSpec((None, MOD_ROWS, tn), lambda l, j: (l, 0, j)),
        out_shape=jax.ShapeDtypeStruct((DEPTH, MOD_ROWS, 6 * D_MODEL), F32),
        compiler_params=_params(("parallel", "parallel")),
        name="adaln",
    )(cvec, w_mod, b_mod.reshape(DEPTH, 1, 6 * D_MODEL))
    return out.reshape(DEPTH, MOD_ROWS, 6, D_MODEL)


def _proj_in_kernel(x_ref, mod_ref, n_ref, w_ref, rec_ref, gate_ref, h_ref):
    j = pl.program_id(2)
    n_rec = N_REC // w_ref.shape[1]
    rows = x_ref.shape[0]
    chunk = min(PROLOGUE_ROWS, rows)

    @pl.when(j == 0)
    def _():
        for r0 in range(0, rows, chunk):
            rs = pl.ds(r0, chunk)
            h = _rms(x_ref[rs, :]) * n_ref[...]
            h = (h * (1.0 + mod_ref[1:2, :]) + mod_ref[0:1, :]).astype(BF16)
            h_ref[rs, :] = h
            rec_ref[rs, :] = jnp.dot(h, w_ref[...], preferred_element_type=F32)

    @pl.when((j > 0) & (j < n_rec))
    def _():
        rec_ref[...] = jnp.dot(h_ref[...], w_ref[...], preferred_element_type=F32)

    @pl.when(j >= n_rec)
    def _():
        gate_ref[...] = jnp.dot(h_ref[...], w_ref[...], preferred_element_type=F32).astype(BF16)


def proj_in(x, mod, norm_w, w_in_bf, layer, row0):
    bsz, n, _ = x.shape
    tm = min(TM_IN, n)
    tn = TN_IN
    n_rec = N_REC // tn
    return pl.pallas_call(
        _proj_in_kernel,
        grid=(bsz, n // tm, N_IN // tn),
        in_specs=[
            pl.BlockSpec((None, tm, D_MODEL), lambda b, i, j: (b, i, 0)),
            pl.BlockSpec((None, None, 6, D_MODEL), lambda b, i, j: (layer, row0 + b, 0, 0)),
            pl.BlockSpec((None, 1, D_MODEL), lambda b, i, j: (layer, 0, 0)),
            pl.BlockSpec((None, D_MODEL, tn), lambda b, i, j: (layer, 0, j)),
        ],
        out_specs=[
            pl.BlockSpec((None, tm, tn), lambda b, i, j: (b, i, jnp.minimum(j, n_rec - 1))),
            pl.BlockSpec((None, tm, tn), lambda b, i, j: (b, i, jnp.maximum(j - n_rec, 0))),
        ],
        out_shape=[jax.ShapeDtypeStruct((bsz, n, N_REC), F32),
                   jax.ShapeDtypeStruct((bsz, n, N_GATE), BF16)],
        scratch_shapes=[pltpu.VMEM((tm, D_MODEL), BF16)],
        compiler_params=_params(("parallel", "parallel", "arbitrary")),
        name="proj_in",
    )(x, mod, norm_w, w_in_bf)


def _nt(a, b):
    return lax.dot_general(a, b, (((1,), (1,)), ((), ())), preferred_element_type=F32)


def _tn(a, b):
    return lax.dot_general(a, b, (((0,), (0,)), ((), ())), preferred_element_type=F32)


def _chunk(q_ref, k_ref, b_ref, v_ref, m_ref, base, sl, st, reverse, pair_masks):
    def rows(ref, r0, n):
        return ref[pl.ds(base + r0, n), sl]

    def row(ref, r):
        return ref[pl.ds(base + r, 1), sl]

    q = rows(q_ref, 0, SCAN_CHUNK)
    k = rows(k_ref, 0, SCAN_CHUNK)
    b = rows(b_ref, 0, SCAN_CHUNK)
    vb = rows(v_ref, 0, SCAN_CHUNK).astype(BF16)
    b_end = row(b_ref, 0 if reverse else SCAN_CHUNK - 1)

    o = _nt((q * jnp.exp2(b)).astype(BF16), st.astype(BF16))

    levels = []
    for h in SCAN_LEVELS:
        z = jnp.zeros((h, LANES), F32)
        qp, kp = [], []
        for r0 in range(0, SCAN_CHUNK, 2 * h):
            if reverse:
                beta = row(b_ref, r0 + h)
                qp += [rows(q_ref, r0, h) * jnp.exp2(rows(b_ref, r0, h) - beta), z]
                kp += [z, rows(k_ref, r0 + h, h) * jnp.exp2(beta - rows(b_ref, r0 + h, h))]
            else:
                beta = row(b_ref, r0 + h - 1)
                kp += [rows(k_ref, r0, h) * jnp.exp2(beta - rows(b_ref, r0, h)), z]
                qp += [z, rows(q_ref, r0 + h, h) * jnp.exp2(rows(b_ref, r0 + h, h) - beta)]
        levels.append(_nt(jnp.concatenate(qp, axis=0).astype(BF16),
                          jnp.concatenate(kp, axis=0).astype(BF16)))

    parts = []
    for r0 in range(0, SCAN_CHUNK, SUBLANES):
        qv = rows(q_ref, r0, SUBLANES)
        bv = rows(b_ref, r0, SUBLANES)
        near = jnp.zeros((SUBLANES, LANES), F32)
        for j in range(SUBLANES):
            shape = (SUBLANES, LANES)
            kj = jnp.broadcast_to(row(k_ref, r0 + j), shape)
            bj = jnp.broadcast_to(row(b_ref, r0 + j), shape)
            w = jnp.sum(qv * kj * jnp.exp2(bv - bj), axis=1, keepdims=True)
            near = jnp.where(pair_masks[j], w, near)
        rs = slice(r0, r0 + SUBLANES)
        acc = near * m_ref[len(SCAN_LEVELS), rs, :]
        for li, h in enumerate(SCAN_LEVELS):
            if ((r0 & h) == 0) == reverse:
                acc = acc + levels[li][rs] * m_ref[li, rs, :]
        parts.append(acc)
    a = jnp.concatenate(parts, axis=0)
    o = o + jnp.dot(a.astype(BF16), vb, preferred_element_type=F32)

    khat = (k * jnp.exp2(b_end - b)).astype(BF16)
    st = st * jnp.exp2(b_end) + _tn(vb, khat)
    return o, st


def _scan_kernel(*refs, layer, reverse, has_s0, has_gate, has_sbuf):
    refs = list(refs)
    q_ref, x_ref, v_ref, lb_ref, m_ref = refs[:5]
    del refs[:5]
    s0_ref = refs.pop(0) if has_s0 else None
    of_ref, og_ref, gn_ref = (refs.pop(0), refs.pop(0), refs.pop(0)) if has_gate else (None, None, None)
    if has_sbuf:
        refs.pop(0)
    o_ref, sfin_ref, st_ref, k_scr, b_scr = refs
    i = pl.program_id(2)
    nch = q_ref.shape[0] // SCAN_CHUNK
    width = q_ref.shape[1]

    @pl.when(i == 0)
    def _():
        for hh in range(SCAN_HEADS):
            if has_s0:
                st_ref[hh] = s0_ref[hh].T
            else:
                st_ref[hh] = jnp.zeros((DV, DK), F32)

    t_idx = lax.broadcasted_iota(jnp.int32, (SCAN_CHUNK, SCAN_CHUNK), 0)
    s_idx = lax.broadcasted_iota(jnp.int32, (SCAN_CHUNK, SCAN_CHUNK), 1)
    incl = (s_idx >= t_idx) if reverse else (s_idx <= t_idx)
    tmat = jnp.where(incl, 1.0, 0.0).astype(BF16)
    sub = lax.broadcasted_iota(jnp.int32, (SUBLANES, LANES), 0)
    lane = lax.broadcasted_iota(jnp.int32, (SUBLANES, LANES), 1)
    pair_masks = [((lane & (SUBLANES - 1)) == j) & ((sub <= j) if reverse else (sub >= j))
                  for j in range(SUBLANES)]

    lbr = lb_ref[...]
    e = jnp.exp(lbr - jnp.max(lbr, axis=0, keepdims=True))
    lb_all = jnp.sum(e[1:layer + 1], axis=0, keepdims=True) / jnp.sum(e, axis=0, keepdims=True)

    for c in (range(nch - 1, -1, -1) if reverse else range(nch)):
        base = SCAN_CHUNK * c
        rs = pl.ds(base, SCAN_CHUNK)
        f = jax.nn.sigmoid(x_ref[rs, :])
        if layer > 0:
            f = lb_all + (1.0 - lb_all) * f
        g = jnp.log(f) * LOG2E
        ghi = g.astype(BF16)
        glo = (g - ghi.astype(F32)).astype(BF16)
        b2 = jnp.dot(tmat, jnp.concatenate([ghi, glo], axis=1), preferred_element_type=F32)
        k_scr[rs, :] = 1.0 - f
        b_scr[rs, :] = b2[:, :width] + b2[:, width:]
        for hh in range(SCAN_HEADS):
            sl = slice(LANES * hh, LANES * (hh + 1))
            o, st = _chunk(q_ref, k_scr, b_scr, v_ref, m_ref, base, sl, st_ref[hh], reverse,
                           pair_masks)
            if has_gate:
                o = _rms(o + of_ref[rs, sl]) * gn_ref[...]
                o_ref[rs, sl] = (o * _silu(og_ref[rs, sl].astype(F32))).astype(BF16)
            else:
                o_ref[rs, sl] = o
            st_ref[hh] = st

    @pl.when(i == pl.num_programs(2) - 1)
    def _():
        for hh in range(SCAN_HEADS):
            sfin_ref[hh] = st_ref[hh].T


def scan_masks(reverse):
    t = jnp.arange(SCAN_CHUNK, dtype=jnp.int32)[:, None]
    s = jnp.arange(SCAN_CHUNK, dtype=jnp.int32)[None, :]
    diff = t ^ s
    strict = (s > t) if reverse else (s < t)
    masks = [(diff >= h) & (diff < 2 * h) & strict for h in SCAN_LEVELS] + [diff < SUBLANES]
    return jnp.stack(masks).astype(F32)


def scan(p, lb_dir, state, layer, direction, gate=None, state_buf=None):
    bsz, n, _ = p.shape
    reverse = direction == 1
    nblk = n // SCAN_ROWS
    wblk = LANES * SCAN_HEADS
    has_s0 = state is not None
    has_gate = gate is not None
    has_sbuf = state_buf is not None

    def rowblk(i):
        return nblk - 1 - i if reverse else i

    def col(c0):
        return lambda b, h, i: (b, rowblk(i), c0 // wblk + h)

    in_specs = [
        pl.BlockSpec((None, SCAN_ROWS, wblk), col(COL_Q)),
        pl.BlockSpec((None, SCAN_ROWS, wblk), col(COL_FB if reverse else COL_FF)),
        pl.BlockSpec((None, SCAN_ROWS, wblk), col(COL_IV)),
        pl.BlockSpec((None, DEPTH, wblk), lambda b, h, i: (direction, 0, h)),
        pl.BlockSpec((len(SCAN_LEVELS) + 1, SCAN_CHUNK, SCAN_CHUNK), lambda b, h, i: (0, 0, 0)),
    ]
    args = [p, p, p, lb_dir, scan_masks(reverse)]
    if has_s0:
        in_specs.append(pl.BlockSpec((None, None, None, SCAN_HEADS, DK, DV),
                                     lambda b, h, i: (b, layer, direction, h, 0, 0)))
        args.append(state)
    if has_gate:
        o_other, p_gate, g_norm = gate
        in_specs += [
            pl.BlockSpec((None, SCAN_ROWS, wblk), lambda b, h, i: (b, rowblk(i), h)),
            pl.BlockSpec((None, SCAN_ROWS, wblk), lambda b, h, i: (b, rowblk(i), GCOL_OG // wblk + h)),
            pl.BlockSpec((None, 1, DV), lambda b, h, i: (layer, 0, 0)),
        ]
        args += [o_other, p_gate, g_norm]
    aliases = {}
    if has_sbuf:
        aliases = {len(args): 1}
        in_specs.append(pl.BlockSpec(memory_space=pl.ANY))
        args.append(state_buf)
        s_spec = pl.BlockSpec((None, None, None, SCAN_HEADS, DK, DV),
                              lambda b, h, i: (b, layer, direction, h, 0, 0))
        s_shape = jax.ShapeDtypeStruct(state_buf.shape, F32)
    else:
        s_spec = pl.BlockSpec((None, SCAN_HEADS, DK, DV), lambda b, h, i: (b, h, 0, 0))
        s_shape = jax.ShapeDtypeStruct((bsz, H_A, DK, DV), F32)
    return pl.pallas_call(
        functools.partial(_scan_kernel, layer=layer, reverse=reverse, has_s0=has_s0,
                          has_gate=has_gate, has_sbuf=has_sbuf),
        grid=(bsz, H_A // SCAN_HEADS, nblk),
        in_specs=in_specs,
        out_specs=[
            pl.BlockSpec((None, SCAN_ROWS, wblk), lambda b, h, i: (b, rowblk(i), h)),
            s_spec,
        ],
        out_shape=[
            jax.ShapeDtypeStruct((bsz, n, W_A), BF16 if has_gate else F32),
            s_shape,
        ],
        input_output_aliases=aliases,
        scratch_shapes=[pltpu.VMEM((SCAN_HEADS, DV, DK), F32),
                        pltpu.VMEM((SCAN_ROWS, wblk), F32),
                        pltpu.VMEM((SCAN_ROWS, wblk), F32)],
        compiler_params=_params(("parallel", "parallel", "arbitrary")),
        name="scan_bwd" if reverse else "scan_fwd",
    )(*args)


def _fourier_direct_kernel(u_ref, cs_ref, c_ref, s_ref, z_ref, *, scale):
    cs = cs_ref[...]
    for i in range(u_ref.shape[0]):
        a_parts, b_parts = [], []
        for g in range(N_FG):
            r = jnp.dot(u_ref[i, :, FG * g:FG * (g + 1)], cs, preferred_element_type=F32)
            a_parts.append(r[:, :FG].astype(BF16))
            b_parts.append(r[:, FG:].astype(BF16))
        acc = jnp.dot(c_ref[...], jnp.concatenate(a_parts, axis=1), preferred_element_type=F32)
        acc = acc - jnp.dot(s_ref[...], jnp.concatenate(b_parts, axis=1), preferred_element_type=F32)
        z_ref[i] = (acc * scale).astype(BF16)


def fourier_direct(p, cs_f, cos_n, sin_n):
    bsz, n, _ = p.shape
    bb = math.gcd(FOURIER_BATCH, bsz)
    return pl.pallas_call(
        functools.partial(_fourier_direct_kernel, scale=1.0 / math.sqrt(n * FG)),
        grid=(bsz // bb,),
        in_specs=[
            pl.BlockSpec((bb, n, W_B), lambda b: (b, 0, GCOL_U // W_B)),
            pl.BlockSpec((FG, 2 * FG), lambda b: (0, 0)),
            pl.BlockSpec((n, n), lambda b: (0, 0)),
            pl.BlockSpec((n, n), lambda b: (0, 0)),
        ],
        out_specs=pl.BlockSpec((bb, n, W_B), lambda b: (b, 0, 0)),
        out_shape=jax.ShapeDtypeStruct((bsz, n, W_B), BF16),
        compiler_params=_params(("parallel",)),
        name="fourier_direct",
    )(p, cs_f, cos_n, sin_n)


def dft_tables(n):
    idx = jnp.arange(n, dtype=jnp.int32)
    ang = ((idx[:, None] * idx[None, :]) % n).astype(F32) * (2.0 * math.pi / n)
    return jnp.cos(ang).astype(BF16), jnp.sin(ang).astype(BF16)


DFT_RADIX = 64
DFT_GROUP = 8
DFT_STEP = 16


def _kron_rows(mat, inner):
    r, c = mat.shape
    eye = jnp.eye(inner, dtype=mat.dtype)
    return (mat[:, None, :, None] * eye[None, :, None, :]).reshape(r * inner, c * inner)


def dft2_constants(n):
    m = DFT_RADIX
    idx = jnp.arange(m, dtype=jnp.int32)
    ang = ((idx[:, None] * idx[None, :]) % m).astype(F32) * (2.0 * math.pi / m)
    cm, sm = jnp.cos(ang), jnp.sin(ang)
    ck, sk = _kron_rows(cm, DFT_GROUP), _kron_rows(sm, DFT_GROUP)
    l1 = jnp.concatenate([jnp.concatenate([ck, -sk], axis=1),
                          jnp.concatenate([sk, ck], axis=1)], axis=0).astype(BF16)
    eye = jnp.eye(DFT_GROUP, dtype=F32)
    c2 = (cm[:, None, None, :] * eye[None, :, :, None]).reshape(m * DFT_GROUP, DFT_GROUP * m)
    s2 = (sm[:, None, None, :] * eye[None, :, :, None]).reshape(m * DFT_GROUP, DFT_GROUP * m)
    l2 = jnp.concatenate([c2, -s2], axis=1).astype(BF16)
    k1 = jnp.arange(m, dtype=jnp.int32)[None, :, None]
    n2 = (jnp.arange(m // DFT_GROUP, dtype=jnp.int32)[:, None, None] * DFT_GROUP
          + jnp.arange(DFT_GROUP, dtype=jnp.int32)[None, None, :])
    tang = ((k1 * n2) % n).astype(F32) * (2.0 * math.pi / n)
    tang = jnp.broadcast_to(tang.reshape(m // DFT_GROUP, m * DFT_GROUP, 1),
                            (m // DFT_GROUP, m * DFT_GROUP, LANES))
    return l1, l2, jnp.cos(tang), jnp.sin(tang)


def _dft_stage1_kernel(u_ref, cs_ref, l1_ref, tc_ref, ts_ref, yr_ref, yi_ref):
    m = DFT_RADIX
    rows = m * DFT_GROUP
    cs = cs_ref[...]
    l1 = l1_ref[...]
    yr_parts, yi_parts = [], []
    u_all = u_ref[...].astype(F32)
    for hf in range(DFT_STEP // DFT_GROUP):
        u = u_all[:, DFT_GROUP * hf:DFT_GROUP * (hf + 1), :].reshape(rows, W_B)
        a_parts, b_parts = [], []
        for g in range(N_FG):
            r = jnp.dot(u[:, FG * g:FG * (g + 1)].astype(BF16), cs, preferred_element_type=F32)
            a_parts.append(r[:, :FG].astype(BF16))
            b_parts.append(r[:, FG:].astype(BF16))
        x = jnp.concatenate([jnp.concatenate(a_parts, axis=1), jnp.concatenate(b_parts, axis=1)], axis=0)
        y = jnp.dot(l1, x, preferred_element_type=F32)
        yr, yi = y[:rows], y[rows:]
        tc, ts = tc_ref[hf], ts_ref[hf]
        pr, pi = [], []
        for c in range(W_B // LANES):
            sl = slice(LANES * c, LANES * (c + 1))
            pr.append(yr[:, sl] * tc - yi[:, sl] * ts)
            pi.append(yr[:, sl] * ts + yi[:, sl] * tc)
        yr_parts.append(jnp.concatenate(pr, axis=1).reshape(m, DFT_GROUP, W_B))
        yi_parts.append(jnp.concatenate(pi, axis=1).reshape(m, DFT_GROUP, W_B))
    yr_ref[...] = jnp.concatenate(yr_parts, axis=1).astype(BF16)
    yi_ref[...] = jnp.concatenate(yi_parts, axis=1).astype(BF16)


def _dft_stage2_kernel(yr_ref, yi_ref, l2_ref, z_ref, *, scale):
    m = DFT_RADIX
    rows = m * DFT_GROUP
    l2 = l2_ref[...]
    parts = []
    for hf in range(DFT_STEP // DFT_GROUP):
        sl = slice(DFT_GROUP * hf, DFT_GROUP * (hf + 1))
        x = jnp.concatenate([yr_ref[sl].reshape(rows, W_B), yi_ref[sl].reshape(rows, W_B)], axis=0)
        z = jnp.dot(l2, x, preferred_element_type=F32) * scale
        parts.append(z.reshape(m, DFT_GROUP, W_B))
    z_ref[...] = jnp.concatenate(parts, axis=1).astype(BF16)


def fourier_two_stage(p, cs_f, consts):
    bsz, n, _ = p.shape
    m = DFT_RADIX
    l1, l2, tw_c, tw_s = consts
    p4 = p.reshape(bsz, m, m, N_GATE)
    nstep = m // DFT_STEP
    per_step = DFT_STEP // DFT_GROUP
    ybuf = jax.ShapeDtypeStruct((bsz, m, m, W_B), BF16)
    yspec = pl.BlockSpec((None, m, DFT_STEP, W_B), lambda b, j: (b, 0, j, 0))
    twspec = pl.BlockSpec((per_step, m * DFT_GROUP, LANES), lambda b, j: (j, 0, 0))
    yr, yi = pl.pallas_call(
        _dft_stage1_kernel,
        grid=(bsz, nstep),
        in_specs=[
            pl.BlockSpec((None, m, DFT_STEP, W_B), lambda b, j: (b, 0, j, GCOL_U // W_B)),
            pl.BlockSpec((FG, 2 * FG), lambda b, j: (0, 0)),
            pl.BlockSpec(l1.shape, lambda b, j: (0, 0)),
            twspec, twspec,
        ],
        out_specs=[yspec, yspec],
        out_shape=[ybuf, ybuf],
        compiler_params=_params(("parallel", "parallel")),
        name="dft_stage1",
    )(p4, cs_f, l1, tw_c, tw_s)
    xspec = pl.BlockSpec((None, DFT_STEP, m, W_B), lambda b, i: (b, i, 0, 0))
    z = pl.pallas_call(
        functools.partial(_dft_stage2_kernel, scale=1.0 / math.sqrt(n * FG)),
        grid=(bsz, nstep),
        in_specs=[xspec, xspec, pl.BlockSpec(l2.shape, lambda b, i: (0, 0))],
        out_specs=pl.BlockSpec((None, m, DFT_STEP, W_B), lambda b, i: (b, 0, i, 0)),
        out_shape=jax.ShapeDtypeStruct((bsz, m, m, W_B), BF16),
        compiler_params=_params(("parallel", "parallel")),
        name="dft_stage2",
    )(yr, yi, l2)
    return z.reshape(bsz, n, W_B)


def _mix_out_kernel(o_ref, z_ref, ga_ref, gb_ref, x_ref, mod_ref, wa_ref, wb_ref, wo_ref, out_ref):
    ya = jnp.dot(o_ref[...], wa_ref[...], preferred_element_type=F32)
    yb = jnp.dot(z_ref[...], wb_ref[...], preferred_element_type=F32)
    merged = (jax.nn.sigmoid(ga_ref[...].astype(F32)) * ya
              + jax.nn.sigmoid(gb_ref[...].astype(F32)) * yb)
    m = jnp.dot(merged.astype(BF16), wo_ref[...], preferred_element_type=F32)
    out_ref[...] = x_ref[...] + mod_ref[2:3, :] * m


def mix_out(o, p_gate, z, x, mod, w_a_bf, w_b_bf, w_out_bf, layer, row0):
    bsz, n, _ = x.shape
    tm = min(TM_MIX, n)

    def tok(width, cblk=0):
        return pl.BlockSpec((None, tm, width), lambda b, i: (b, i, cblk))

    def whole(rows, cols):
        return pl.BlockSpec((None, rows, cols), lambda b, i: (layer, 0, 0),
                            pipeline_mode=pl.Buffered(1))

    in_specs = [tok(W_A), tok(W_B), tok(D_MODEL, GCOL_GA // D_MODEL), tok(D_MODEL, GCOL_GB // D_MODEL),
                tok(D_MODEL)]
    args = [o, z, p_gate, p_gate, x]
    in_specs += [
        pl.BlockSpec((None, None, 6, D_MODEL), lambda b, i: (layer, row0 + b, 0, 0)),
        whole(W_A, D_MODEL), whole(W_B, D_MODEL), whole(D_MODEL, D_MODEL),
    ]
    args += [mod, w_a_bf, w_b_bf, w_out_bf]
    return pl.pallas_call(
        _mix_out_kernel,
        grid=(bsz, n // tm),
        in_specs=in_specs,
        out_specs=tok(D_MODEL),
        out_shape=jax.ShapeDtypeStruct((bsz, n, D_MODEL), F32),
        compiler_params=_params(("parallel", "parallel")),
        name="mix_out",
    )(*args)


def _ffn_kernel(*refs, has_final):
    if has_final:
        x_ref, mod_ref, n_ref, w1a_ref, w1g_ref, w2_ref, nf_ref, out_ref, h_ref = refs
    else:
        x_ref, mod_ref, n_ref, w1a_ref, w1g_ref, w2_ref, out_ref, h_ref = refs
    j = pl.program_id(2)
    rows = x_ref.shape[0]
    chunk = min(PROLOGUE_ROWS, rows)

    def partial_out(h):
        a = jnp.dot(h, w1a_ref[...], preferred_element_type=F32)
        gt = jnp.dot(h, w1g_ref[...], preferred_element_type=F32)
        return jnp.dot((_silu(a) * gt).astype(BF16), w2_ref[...], preferred_element_type=F32)

    @pl.when(j == 0)
    def _():
        for r0 in range(0, rows, chunk):
            rs = pl.ds(r0, chunk)
            h = _rms(x_ref[rs, :]) * n_ref[...]
            h = (h * (1.0 + mod_ref[4:5, :]) + mod_ref[3:4, :]).astype(BF16)
            h_ref[rs, :] = h
            out_ref[rs, :] = partial_out(h)

    last = pl.num_programs(2) - 1

    @pl.when((j > 0) & (j < last))
    def _():
        out_ref[...] += partial_out(h_ref[...])

    @pl.when(j == last)
    def _():
        for r0 in range(0, rows, chunk):
            rs = pl.ds(r0, chunk)
            acc = out_ref[rs, :] + partial_out(h_ref[rs, :])
            y = x_ref[rs, :] + mod_ref[5:6, :] * acc
            if has_final:
                y = _rms(y) * nf_ref[...]
            out_ref[rs, :] = y


def ffn(x, mod, norm_w, w_ff_in_bf, w_ff_out_bf, norm_final, layer, row0):
    bsz, n, _ = x.shape
    tm = min(TM_FFN, n)
    nj = D_FF // TN_FFN
    has_final = norm_final is not None
    in_specs = [
        pl.BlockSpec((None, tm, D_MODEL), lambda b, i, j: (b, i, 0)),
        pl.BlockSpec((None, None, 6, D_MODEL), lambda b, i, j: (layer, row0 + b, 0, 0)),
        pl.BlockSpec((None, 1, D_MODEL), lambda b, i, j: (layer, 0, 0)),
        pl.BlockSpec((None, D_MODEL, TN_FFN), lambda b, i, j: (layer, 0, j)),
        pl.BlockSpec((None, D_MODEL, TN_FFN), lambda b, i, j: (layer, 0, nj + j)),
        pl.BlockSpec((None, TN_FFN, D_MODEL), lambda b, i, j: (layer, j, 0)),
    ]
    args = [x, mod, norm_w, w_ff_in_bf, w_ff_in_bf, w_ff_out_bf]
    if has_final:
        in_specs.append(pl.BlockSpec((1, D_MODEL), lambda b, i, j: (0, 0)))
        args.append(norm_final)
    return pl.pallas_call(
        functools.partial(_ffn_kernel, has_final=has_final),
        grid=(bsz, n // tm, nj),
        in_specs=in_specs,
        out_specs=pl.BlockSpec((None, tm, D_MODEL), lambda b, i, j: (b, i, 0)),
        out_shape=jax.ShapeDtypeStruct((bsz, n, D_MODEL), F32),
        scratch_shapes=[pltpu.VMEM((tm, D_MODEL), BF16)],
        compiler_params=_params(("parallel", "parallel", "arbitrary")),
        name="ffn",
    )(*args)


def grid_pos_embed(n):
    rows = n // GRID_W
    quarter = D_MODEL // 4
    omega = 1.0 / (POS_BASE ** (jnp.arange(quarter, dtype=F32) / quarter))

    def enc(count):
        a = jnp.arange(count, dtype=F32)[:, None] * omega[None, :]
        return jnp.concatenate([jnp.sin(a), jnp.cos(a)], axis=-1)

    er = jnp.broadcast_to(enc(rows)[:, None, :], (rows, GRID_W, 2 * quarter))
    ec = jnp.broadcast_to(enc(GRID_W)[None, :, :], (rows, GRID_W, 2 * quarter))
    return jnp.concatenate([er, ec], axis=-1).reshape(n, D_MODEL)


def _add_pos_kernel(x_ref, pe_ref, out_ref):
    out_ref[...] = x_ref[...] + pe_ref[...]


def add_pos(x, pe):
    bsz, n, _ = x.shape
    tm = min(TM_IN, n)
    return pl.pallas_call(
        _add_pos_kernel,
        grid=(n // tm, bsz),
        in_specs=[pl.BlockSpec((None, tm, D_MODEL), lambda i, b: (b, i, 0)),
                  pl.BlockSpec((tm, D_MODEL), lambda i, b: (i, 0))],
        out_specs=pl.BlockSpec((None, tm, D_MODEL), lambda i, b: (b, i, 0)),
        out_shape=jax.ShapeDtypeStruct(x.shape, x.dtype),
        compiler_params=_params(("parallel", "parallel")),
        name="add_pos",
    )(x, pe)


def kernel(x_prompt, x_sample, state_hgrn, c, c_ctx, w_mod, b_mod, norm_mix, norm_ffn, w_in, lb_raw,
           g_norm, w_a, w_b, w_out, w_ff_in, w_ff_out, norm_final):
    bc, nc, _ = x_prompt.shape
    bl, nl, _ = x_sample.shape
    assert 1 + bl <= MOD_ROWS

    cvec = jnp.zeros((MOD_ROWS, D_MODEL), F32).at[0].set(c_ctx).at[1:1 + bl].set(c)
    mod = adaln_table(cvec, w_mod, b_mod)

    w_in_bf, w_a_bf, w_b_bf, w_out_bf = (w.astype(BF16) for w in (w_in, w_a, w_b, w_out))
    w_ff_in_bf, w_ff_out_bf = w_ff_in.astype(BF16), w_ff_out.astype(BF16)
    norm_mix3 = norm_mix.reshape(DEPTH, 1, D_MODEL)
    norm_ffn3 = norm_ffn.reshape(DEPTH, 1, D_MODEL)
    g_norm3 = g_norm.reshape(DEPTH, 1, DV)
    norm_final2 = norm_final.reshape(1, D_MODEL)
    lb_dir = jnp.transpose(lb_raw, (1, 0, 2))

    fidx = jnp.arange(FG, dtype=jnp.int32)
    fang = ((fidx[:, None] * fidx[None, :]) % FG).astype(F32) * (2.0 * math.pi / FG)
    cs_f = jnp.concatenate([jnp.cos(fang), jnp.sin(fang)], axis=1).astype(BF16)
    tables = {n: dft2_constants(n) if n == DFT_RADIX ** 2 else dft_tables(n) for n in {nc, nl}}

    def layer(x, seq_shape, state, sbuf, l, row0):
        bt, nt, _ = x.shape
        bs, ns = seq_shape
        p_rec, p_gate = proj_in(x, mod, norm_mix3, w_in_bf, l, row0)
        rec_s = p_rec.reshape(bs, ns, N_REC)
        gate_s = p_gate.reshape(bs, ns, N_GATE)
        o_f, s_f = scan(rec_s, lb_dir, state, l, 0, state_buf=sbuf)
        if sbuf is not None:
            sbuf = s_f
        o, s_b = scan(rec_s, lb_dir, state, l, 1, gate=(o_f, gate_s, g_norm3), state_buf=sbuf)
        if sbuf is not None:
            sbuf = s_b
        if ns == DFT_RADIX ** 2:
            z = fourier_two_stage(gate_s, cs_f, tables[ns])
        else:
            z = fourier_direct(gate_s, cs_f, *tables[ns])
        x = mix_out(o.reshape(bt, nt, W_A), p_gate, z.reshape(bt, nt, W_B),
                    x, mod, w_a_bf, w_b_bf, w_out_bf, l, row0)
        x = ffn(x, mod, norm_ffn3, w_ff_in_bf, w_ff_out_bf,
                norm_final2 if l == DEPTH - 1 else None, l, row0)
        return x, sbuf

    xc = x_prompt.reshape(1, bc * nc, D_MODEL)
    xs = add_pos(x_sample, grid_pos_embed(nl))
    state_new = jnp.zeros((bc, DEPTH, 2, H_A, DK, DV), F32)
    for l in range(DEPTH):
        xc, state_new = layer(xc, (bc, nc), None, state_new, l, 0)
        xs, _ = layer(xs, (bl, nl), state_hgrn, None, l, 1)
    return xc.reshape(bc, nc, D_MODEL), xs, state_new.astype(x_prompt.dtype)
```

```python
import functools
import math

import jax
import jax.numpy as jnp
from jax import lax
from jax.experimental import pallas as pl
from jax.experimental.pallas import tpu as pltpu

D_MODEL = 2048
DEPTH = 2
H_A = 8
DK = 128
DV = 128
W_A = H_A * DK
N_FG = 4
FG = 256
W_B = N_FG * FG
D_FF = 5632
N_IN = 5 * W_A + W_B + 2 * D_MODEL
EPS = 1e-6
GRID_W = 64
POS_BASE = 10000.0

N_REC = 4 * W_A
N_GATE = N_IN - N_REC
COL_Q = 0
COL_FF = W_A
COL_FB = 2 * W_A
COL_IV = 3 * W_A
GCOL_OG = 0
GCOL_U = W_A
GCOL_GA = W_A + W_B
GCOL_GB = W_A + W_B + D_MODEL

MOD_ROWS = 8
LANES = 128
SUBLANES = 8
VMEM_LIMIT = 56 * 1024 * 1024

TM_IN, TN_IN = 1024, 1024
PROLOGUE_ROWS = 256
TM_MIX = 256
TM_FFN, TN_FFN = 1024, 512
FOURIER_BATCH = 4
SCAN_ROWS = 512
SCAN_CHUNK = 128
SCAN_LEVELS = (64, 32, 16, 8)
SCAN_HEADS = 8
LOG2E = 1.4426950408889634

F32 = jnp.float32
BF16 = jnp.bfloat16


def _params(sem):
    return pltpu.CompilerParams(dimension_semantics=sem, vmem_limit_bytes=VMEM_LIMIT)


def _silu(x):
    return x * jax.nn.sigmoid(x)


def _rms(x):
    return x * lax.rsqrt(jnp.mean(x * x, axis=-1, keepdims=True) + EPS)


def _adaln_kernel(c_ref, w_ref, b_ref, out_ref):
    s = _silu(c_ref[...]).astype(BF16)
    out_ref[...] = jnp.dot(s, w_ref[...].astype(BF16), preferred_element_type=F32) + b_ref[...]


def adaln_table(cvec, w_mod, b_mod):
    tn = 1024
    out = pl.pallas_call(
        _adaln_kernel,
        grid=(DEPTH, 6 * D_MODEL // tn),
        in_specs=[
            pl.BlockSpec((MOD_ROWS, D_MODEL), lambda l, j: (0, 0)),
            pl.BlockSpec((None, D_MODEL, tn), lambda l, j: (l, 0, j)),
            pl.BlockSpec((None, 1, tn), lambda l, j: (l, 0, j)),
        ],
        out_specs=pl.BlockSpec((None, MOD_ROWS, tn), lambda l, j: (l, 0, j)),
        out_shape=jax.ShapeDtypeStruct((DEPTH, MOD_ROWS, 6 * D_MODEL), F32),
        compiler_params=_params(("parallel", "parallel")),
        name="adaln",
    )(cvec, w_mod, b_mod.reshape(DEPTH, 1, 6 * D_MODEL))
    return out.reshape(DEPTH, MOD_ROWS, 6, D_MODEL)


def _proj_in_kernel(x_ref, mod_ref, n_ref, w_ref, rec_ref, gate_ref, h_ref):
    j = pl.program_id(2)
    n_rec = N_REC // w_ref.shape[1]
    rows = x_ref.shape[0]
    chunk = min(PROLOGUE_ROWS, rows)

    @pl.when(j == 0)
    def _():
        for r0 in range(0, rows, chunk):
            rs = pl.ds(r0, chunk)
            h = _rms(x_ref[rs, :]) * n_ref[...]
            h = (h * (1.0 + mod_ref[1:2, :]) + mod_ref[0:1, :]).astype(BF16)
            h_ref[rs, :] = h
            rec_ref[rs, :] = jnp.dot(h, w_ref[...], preferred_element_type=F32)

    @pl.when((j > 0) & (j < n_rec))
    def _():
        rec_ref[...] = jnp.dot(h_ref[...], w_ref[...], preferred_element_type=F32)

    @pl.when(j >= n_rec)
    def _():
        gate_ref[...] = jnp.dot(h_ref[...], w_ref[...], preferred_element_type=F32).astype(BF16)


def proj_in(x, mod, norm_w, w_in_bf, layer, row0):
    bsz, n, _ = x.shape
    tm = min(TM_IN, n)
    tn = TN_IN
    n_rec = N_REC // tn
    return pl.pallas_call(
        _proj_in_kernel,
        grid=(bsz, n // tm, N_IN // tn),
        in_specs=[
            pl.BlockSpec((None, tm, D_MODEL), lambda b, i, j: (b, i, 0)),
            pl.BlockSpec((None, None, 6, D_MODEL), lambda b, i, j: (layer, row0 + b, 0, 0)),
            pl.BlockSpec((None, 1, D_MODEL), lambda b, i, j: (layer, 0, 0)),
            pl.BlockSpec((None, D_MODEL, tn), lambda b, i, j: (layer, 0, j)),
        ],
        out_specs=[
            pl.BlockSpec((None, tm, tn), lambda b, i, j: (b, i, jnp.minimum(j, n_rec - 1))),
            pl.BlockSpec((None, tm, tn), lambda b, i, j: (b, i, jnp.maximum(j - n_rec, 0))),
        ],
        out_shape=[jax.ShapeDtypeStruct((bsz, n, N_REC), F32),
                   jax.ShapeDtypeStruct((bsz, n, N_GATE), BF16)],
        scratch_shapes=[pltpu.VMEM((tm, D_MODEL), BF16)],
        compiler_params=_params(("parallel", "parallel", "arbitrary")),
        name="proj_in",
    )(x, mod, norm_w, w_in_bf)


def _nt(a, b):
    return lax.dot_general(a, b, (((1,), (1,)), ((), ())), preferred_element_type=F32)


def _tn(a, b):
    return lax.dot_general(a, b, (((0,), (0,)), ((), ())), preferred_element_type=F32)


def _chunk(q_ref, k_ref, b_ref, v_ref, m_ref, base, sl, st, reverse, pair_masks):
    def rows(ref, r0, n):
        return ref[pl.ds(base + r0, n), sl]

    def row(ref, r):
        return ref[pl.ds(base + r, 1), sl]

    q = rows(q_ref, 0, SCAN_CHUNK)
    k = rows(k_ref, 0, SCAN_CHUNK)
    b = rows(b_ref, 0, SCAN_CHUNK)
    vb = rows(v_ref, 0, SCAN_CHUNK).astype(BF16)
    b_end = row(b_ref, 0 if reverse else SCAN_CHUNK - 1)

    o = _nt((q * jnp.exp2(b)).astype(BF16), st.astype(BF16))

    levels = []
    for h in SCAN_LEVELS:
        z = jnp.zeros((h, LANES), F32)
        qp, kp = [], []
        for r0 in range(0, SCAN_CHUNK, 2 * h):
            if reverse:
                beta = row(b_ref, r0 + h)
                qp += [rows(q_ref, r0, h) * jnp.exp2(rows(b_ref, r0, h) - beta), z]
                kp += [z, rows(k_ref, r0 + h, h) * jnp.exp2(beta - rows(b_ref, r0 + h, h))]
            else:
                beta = row(b_ref, r0 + h - 1)
                kp += [rows(k_ref, r0, h) * jnp.exp2(beta - rows(b_ref, r0, h)), z]
                qp += [z, rows(q_ref, r0 + h, h) * jnp.exp2(rows(b_ref, r0 + h, h) - beta)]
        levels.append(_nt(jnp.concatenate(qp, axis=0).astype(BF16),
                          jnp.concatenate(kp, axis=0).astype(BF16)))

    parts = []
    for r0 in range(0, SCAN_CHUNK, SUBLANES):
        qv = rows(q_ref, r0, SUBLANES)
        bv = rows(b_ref, r0, SUBLANES)
        near = jnp.zeros((SUBLANES, LANES), F32)
        for j in range(SUBLANES):
            shape = (SUBLANES, LANES)
            kj = jnp.broadcast_to(row(k_ref, r0 + j), shape)
            bj = jnp.broadcast_to(row(b_ref, r0 + j), shape)
            w = jnp.sum(qv * kj * jnp.exp2(bv - bj), axis=1, keepdims=True)
            near = jnp.where(pair_masks[j], w, near)
        rs = slice(r0, r0 + SUBLANES)
        acc = near * m_ref[len(SCAN_LEVELS), rs, :]
        for li, h in enumerate(SCAN_LEVELS):
            if ((r0 & h) == 0) == reverse:
                acc = acc + levels[li][rs] * m_ref[li, rs, :]
        parts.append(acc)
    a = jnp.concatenate(parts, axis=0)
    o = o + jnp.dot(a.astype(BF16), vb, preferred_element_type=F32)

    khat = (k * jnp.exp2(b_end - b)).astype(BF16)
    st = st * jnp.exp2(b_end) + _tn(vb, khat)
    return o, st


def _scan_kernel(*refs, layer, reverse, has_s0, has_gate, has_sbuf):
    refs = list(refs)
    q_ref, x_ref, v_ref, lb_ref, m_ref = refs[:5]
    del refs[:5]
    s0_ref = refs.pop(0) if has_s0 else None
    of_ref, og_ref, gn_ref = (refs.pop(0), refs.pop(0), refs.pop(0)) if has_gate else (None, None, None)
    if has_sbuf:
        refs.pop(0)
    o_ref, sfin_ref, st_ref, k_scr, b_scr = refs
    i = pl.program_id(2)
    nch = q_ref.shape[0] // SCAN_CHUNK
    width = q_ref.shape[1]

    @pl.when(i == 0)
    def _():
        for hh in range(SCAN_HEADS):
            if has_s0:
                st_ref[hh] = s0_ref[hh].T
            else:
                st_ref[hh] = jnp.zeros((DV, DK), F32)

    t_idx = lax.broadcasted_iota(jnp.int32, (SCAN_CHUNK, SCAN_CHUNK), 0)
    s_idx = lax.broadcasted_iota(jnp.int32, (SCAN_CHUNK, SCAN_CHUNK), 1)
    incl = (s_idx >= t_idx) if reverse else (s_idx <= t_idx)
    tmat = jnp.where(incl, 1.0, 0.0).astype(BF16)
    sub = lax.broadcasted_iota(jnp.int32, (SUBLANES, LANES), 0)
    lane = lax.broadcasted_iota(jnp.int32, (SUBLANES, LANES), 1)
    pair_masks = [((lane & (SUBLANES - 1)) == j) & ((sub <= j) if reverse else (sub >= j))
                  for j in range(SUBLANES)]

    lbr = lb_ref[...]
    e = jnp.exp(lbr - jnp.max(lbr, axis=0, keepdims=True))
    lb_all = jnp.sum(e[1:layer + 1], axis=0, keepdims=True) / jnp.sum(e, axis=0, keepdims=True)

    for c in (range(nch - 1, -1, -1) if reverse else range(nch)):
        base = SCAN_CHUNK * c
        rs = pl.ds(base, SCAN_CHUNK)
        f = jax.nn.sigmoid(x_ref[rs, :])
        if layer > 0:
            f = lb_all + (1.0 - lb_all) * f
        g = jnp.log(f) * LOG2E
        ghi = g.astype(BF16)
        glo = (g - ghi.astype(F32)).astype(BF16)
        b2 = jnp.dot(tmat, jnp.concatenate([ghi, glo], axis=1), preferred_element_type=F32)
        k_scr[rs, :] = 1.0 - f
        b_scr[rs, :] = b2[:, :width] + b2[:, width:]
        for hh in range(SCAN_HEADS):
            sl = slice(LANES * hh, LANES * (hh + 1))
            o, st = _chunk(q_ref, k_scr, b_scr, v_ref, m_ref, base, sl, st_ref[hh], reverse,
                           pair_masks)
            if has_gate:
                o = _rms(o + of_ref[rs, sl]) * gn_ref[...]
                o_ref[rs, sl] = (o * _silu(og_ref[rs, sl].astype(F32))).astype(BF16)
            else:
                o_ref[rs, sl] = o
            st_ref[hh] = st

    @pl.when(i == pl.num_programs(2) - 1)
    def _():
        for hh in range(SCAN_HEADS):
            sfin_ref[hh] = st_ref[hh].T


def scan_masks(reverse):
    t = jnp.arange(SCAN_CHUNK, dtype=jnp.int32)[:, None]
    s = jnp.arange(SCAN_CHUNK, dtype=jnp.int32)[None, :]
    diff = t ^ s
    strict = (s > t) if reverse else (s < t)
    masks = [(diff >= h) & (diff < 2 * h) & strict for h in SCAN_LEVELS] + [diff < SUBLANES]
    return jnp.stack(masks).astype(F32)


def scan(p, lb_dir, state, layer, direction, gate=None, state_buf=None):
    bsz, n, _ = p.shape
    reverse = direction == 1
    srows = min(SCAN_ROWS, n)
    nblk = n // srows
    wblk = LANES * SCAN_HEADS
    has_s0 = state is not None
    has_gate = gate is not None
    has_sbuf = state_buf is not None

    def rowblk(i):
        return nblk - 1 - i if reverse else i

    def col(c0):
        return lambda b, h, i: (b, rowblk(i), c0 // wblk + h)

    in_specs = [
        pl.BlockSpec((None, srows, wblk), col(COL_Q)),
        pl.BlockSpec((None, srows, wblk), col(COL_FB if reverse else COL_FF)),
        pl.BlockSpec((None, srows, wblk), col(COL_IV)),
        pl.BlockSpec((None, DEPTH, wblk), lambda b, h, i: (direction, 0, h)),
        pl.BlockSpec((len(SCAN_LEVELS) + 1, SCAN_CHUNK, SCAN_CHUNK), lambda b, h, i: (0, 0, 0)),
    ]
    args = [p, p, p, lb_dir, scan_masks(reverse)]
    if has_s0:
        in_specs.append(pl.BlockSpec((None, None, None, SCAN_HEADS, DK, DV),
                                     lambda b, h, i: (b, layer, direction, h, 0, 0)))
        args.append(state)
    if has_gate:
        o_other, p_gate, g_norm = gate
        in_specs += [
            pl.BlockSpec((None, srows, wblk), lambda b, h, i: (b, rowblk(i), h)),
            pl.BlockSpec((None, srows, wblk), lambda b, h, i: (b, rowblk(i), GCOL_OG // wblk + h)),
            pl.BlockSpec((None, 1, DV), lambda b, h, i: (layer, 0, 0)),
        ]
        args += [o_other, p_gate, g_norm]
    aliases = {}
    if has_sbuf:
        aliases = {len(args): 1}
        in_specs.append(pl.BlockSpec(memory_space=pl.ANY))
        args.append(state_buf)
        s_spec = pl.BlockSpec((None, None, None, SCAN_HEADS, DK, DV),
                              lambda b, h, i: (b, layer, direction, h, 0, 0))
        s_shape = jax.ShapeDtypeStruct(state_buf.shape, F32)
    else:
        s_spec = pl.BlockSpec((None, SCAN_HEADS, DK, DV), lambda b, h, i: (b, h, 0, 0))
        s_shape = jax.ShapeDtypeStruct((bsz, H_A, DK, DV), F32)
    return pl.pallas_call(
        functools.partial(_scan_kernel, layer=layer, reverse=reverse, has_s0=has_s0,
                          has_gate=has_gate, has_sbuf=has_sbuf),
        grid=(bsz, H_A // SCAN_HEADS, nblk),
        in_specs=in_specs,
        out_specs=[
            pl.BlockSpec((None, srows, wblk), lambda b, h, i: (b, rowblk(i), h)),
            s_spec,
        ],
        out_shape=[
            jax.ShapeDtypeStruct((bsz, n, W_A), BF16 if has_gate else F32),
            s_shape,
        ],
        input_output_aliases=aliases,
        scratch_shapes=[pltpu.VMEM((SCAN_HEADS, DV, DK), F32),
                        pltpu.VMEM((srows, wblk), F32),
                        pltpu.VMEM((srows, wblk), F32)],
        compiler_params=_params(("parallel", "parallel", "arbitrary")),
        name="scan_bwd" if reverse else "scan_fwd",
    )(*args)


def _fourier_direct_kernel(u_ref, cs_ref, c_ref, s_ref, z_ref, *, scale):
    cs = cs_ref[...]
    for i in range(u_ref.shape[0]):
        a_parts, b_parts = [], []
        for g in range(N_FG):
            r = jnp.dot(u_ref[i, :, FG * g:FG * (g + 1)], cs, preferred_element_type=F32)
            a_parts.append(r[:, :FG].astype(BF16))
            b_parts.append(r[:, FG:].astype(BF16))
        acc = jnp.dot(c_ref[...], jnp.concatenate(a_parts, axis=1), preferred_element_type=F32)
        acc = acc - jnp.dot(s_ref[...], jnp.concatenate(b_parts, axis=1), preferred_element_type=F32)
        z_ref[i] = (acc * scale).astype(BF16)


def fourier_direct(p, cs_f, cos_n, sin_n):
    bsz, n, _ = p.shape
    bb = math.gcd(FOURIER_BATCH, bsz)
    return pl.pallas_call(
        functools.partial(_fourier_direct_kernel, scale=1.0 / math.sqrt(n * FG)),
        grid=(bsz // bb,),
        in_specs=[
            pl.BlockSpec((bb, n, W_B), lambda b: (b, 0, GCOL_U // W_B)),
            pl.BlockSpec((FG, 2 * FG), lambda b: (0, 0)),
            pl.BlockSpec((n, n), lambda b: (0, 0)),
            pl.BlockSpec((n, n), lambda b: (0, 0)),
        ],
        out_specs=pl.BlockSpec((bb, n, W_B), lambda b: (b, 0, 0)),
        out_shape=jax.ShapeDtypeStruct((bsz, n, W_B), BF16),
        compiler_params=_params(("parallel",)),
        name="fourier_direct",
    )(p, cs_f, cos_n, sin_n)


def dft_tables(n):
    idx = jnp.arange(n, dtype=jnp.int32)
    ang = ((idx[:, None] * idx[None, :]) % n).astype(F32) * (2.0 * math.pi / n)
    return jnp.cos(ang).astype(BF16), jnp.sin(ang).astype(BF16)


DFT_RADIX = 64
DFT_GROUP = 8
DFT_STEP = 16


def _kron_rows(mat, inner):
    r, c = mat.shape
    eye = jnp.eye(inner, dtype=mat.dtype)
    return (mat[:, None, :, None] * eye[None, :, None, :]).reshape(r * inner, c * inner)


def dft2_constants(n):
    m = DFT_RADIX
    idx = jnp.arange(m, dtype=jnp.int32)
    ang = ((idx[:, None] * idx[None, :]) % m).astype(F32) * (2.0 * math.pi / m)
    cm, sm = jnp.cos(ang), jnp.sin(ang)
    ck, sk = _kron_rows(cm, DFT_GROUP), _kron_rows(sm, DFT_GROUP)
    l1 = jnp.concatenate([jnp.concatenate([ck, -sk], axis=1),
                          jnp.concatenate([sk, ck], axis=1)], axis=0).astype(BF16)
    eye = jnp.eye(DFT_GROUP, dtype=F32)
    c2 = (cm[:, None, None, :] * eye[None, :, :, None]).reshape(m * DFT_GROUP, DFT_GROUP * m)
    s2 = (sm[:, None, None, :] * eye[None, :, :, None]).reshape(m * DFT_GROUP, DFT_GROUP * m)
    l2 = jnp.concatenate([c2, -s2], axis=1).astype(BF16)
    k1 = jnp.arange(m, dtype=jnp.int32)[None, :, None]
    n2 = (jnp.arange(m // DFT_GROUP, dtype=jnp.int32)[:, None, None] * DFT_GROUP
          + jnp.arange(DFT_GROUP, dtype=jnp.int32)[None, None, :])
    tang = ((k1 * n2) % n).astype(F32) * (2.0 * math.pi / n)
    tang = jnp.broadcast_to(tang.reshape(m // DFT_GROUP, m * DFT_GROUP, 1),
                            (m // DFT_GROUP, m * DFT_GROUP, LANES))
    return l1, l2, jnp.cos(tang), jnp.sin(tang)


def _dft_stage1_kernel(u_ref, cs_ref, l1_ref, tc_ref, ts_ref, yr_ref, yi_ref):
    m = DFT_RADIX
    rows = m * DFT_GROUP
    cs = cs_ref[...]
    l1 = l1_ref[...]
    yr_parts, yi_parts = [], []
    u_all = u_ref[...].astype(F32)
    for hf in range(DFT_STEP // DFT_GROUP):
        u = u_all[:, DFT_GROUP * hf:DFT_GROUP * (hf + 1), :].reshape(rows, W_B)
        a_parts, b_parts = [], []
        for g in range(N_FG):
            r = jnp.dot(u[:, FG * g:FG * (g + 1)].astype(BF16), cs, preferred_element_type=F32)
            a_parts.append(r[:, :FG].astype(BF16))
            b_parts.append(r[:, FG:].astype(BF16))
        x = jnp.concatenate([jnp.concatenate(a_parts, axis=1), jnp.concatenate(b_parts, axis=1)], axis=0)
        y = jnp.dot(l1, x, preferred_element_type=F32)
        yr, yi = y[:rows], y[rows:]
        tc, ts = tc_ref[hf], ts_ref[hf]
        pr, pi = [], []
        for c in range(W_B // LANES):
            sl = slice(LANES * c, LANES * (c + 1))
            pr.append(yr[:, sl] * tc - yi[:, sl] * ts)
            pi.append(yr[:, sl] * ts + yi[:, sl] * tc)
        yr_parts.append(jnp.concatenate(pr, axis=1).reshape(m, DFT_GROUP, W_B))
        yi_parts.append(jnp.concatenate(pi, axis=1).reshape(m, DFT_GROUP, W_B))
    yr_ref[...] = jnp.concatenate(yr_parts, axis=1).astype(BF16)
    yi_ref[...] = jnp.concatenate(yi_parts, axis=1).astype(BF16)


def _dft_stage2_kernel(yr_ref, yi_ref, l2_ref, z_ref, *, scale):
    m = DFT_RADIX
    rows = m * DFT_GROUP
    l2 = l2_ref[...]
    parts = []
    for hf in range(DFT_STEP // DFT_GROUP):
        sl = slice(DFT_GROUP * hf, DFT_GROUP * (hf + 1))
        x = jnp.concatenate([yr_ref[sl].reshape(rows, W_B), yi_ref[sl].reshape(rows, W_B)], axis=0)
        z = jnp.dot(l2, x, preferred_element_type=F32) * scale
        parts.append(z.reshape(m, DFT_GROUP, W_B))
    z_ref[...] = jnp.concatenate(parts, axis=1).astype(BF16)


def fourier_two_stage(p, cs_f, consts):
    bsz, n, _ = p.shape
    m = DFT_RADIX
    l1, l2, tw_c, tw_s = consts
    p4 = p.reshape(bsz, m, m, N_GATE)
    nstep = m // DFT_STEP
    per_step = DFT_STEP // DFT_GROUP
    ybuf = jax.ShapeDtypeStruct((bsz, m, m, W_B), BF16)
    yspec = pl.BlockSpec((None, m, DFT_STEP, W_B), lambda b, j: (b, 0, j, 0))
    twspec = pl.BlockSpec((per_step, m * DFT_GROUP, LANES), lambda b, j: (j, 0, 0))
    yr, yi = pl.pallas_call(
        _dft_stage1_kernel,
        grid=(bsz, nstep),
        in_specs=[
            pl.BlockSpec((None, m, DFT_STEP, W_B), lambda b, j: (b, 0, j, GCOL_U // W_B)),
            pl.BlockSpec((FG, 2 * FG), lambda b, j: (0, 0)),
            pl.BlockSpec(l1.shape, lambda b, j: (0, 0)),
            twspec, twspec,
        ],
        out_specs=[yspec, yspec],
        out_shape=[ybuf, ybuf],
        compiler_params=_params(("parallel", "parallel")),
        name="dft_stage1",
    )(p4, cs_f, l1, tw_c, tw_s)
    xspec = pl.BlockSpec((None, DFT_STEP, m, W_B), lambda b, i: (b, i, 0, 0))
    z = pl.pallas_call(
        functools.partial(_dft_stage2_kernel, scale=1.0 / math.sqrt(n * FG)),
        grid=(bsz, nstep),
        in_specs=[xspec, xspec, pl.BlockSpec(l2.shape, lambda b, i: (0, 0))],
        out_specs=pl.BlockSpec((None, m, DFT_STEP, W_B), lambda b, i: (b, 0, i, 0)),
        out_shape=jax.ShapeDtypeStruct((bsz, m, m, W_B), BF16),
        compiler_params=_params(("parallel", "parallel")),
        name="dft_stage2",
    )(yr, yi, l2)
    return z.reshape(bsz, n, W_B)


def _mix_out_kernel(o_ref, z_ref, ga_ref, gb_ref, x_ref, mod_ref, wa_ref, wb_ref, wo_ref, out_ref):
    ya = jnp.dot(o_ref[...], wa_ref[...], preferred_element_type=F32)
    yb = jnp.dot(z_ref[...], wb_ref[...], preferred_element_type=F32)
    merged = (jax.nn.sigmoid(ga_ref[...].astype(F32)) * ya
              + jax.nn.sigmoid(gb_ref[...].astype(F32)) * yb)
    m = jnp.dot(merged.astype(BF16), wo_ref[...], preferred_element_type=F32)
    out_ref[...] = x_ref[...] + mod_ref[2:3, :] * m


def mix_out(o, p_gate, z, x, mod, w_a_bf, w_b_bf, w_out_bf, layer, row0):
    bsz, n, _ = x.shape
    tm = min(TM_MIX, n)

    def tok(width, cblk=0):
        return pl.BlockSpec((None, tm, width), lambda b, i: (b, i, cblk))

    def whole(rows, cols):
        return pl.BlockSpec((None, rows, cols), lambda b, i: (layer, 0, 0),
                            pipeline_mode=pl.Buffered(1))

    in_specs = [tok(W_A), tok(W_B), tok(D_MODEL, GCOL_GA // D_MODEL), tok(D_MODEL, GCOL_GB // D_MODEL),
                tok(D_MODEL)]
    args = [o, z, p_gate, p_gate, x]
    in_specs += [
        pl.BlockSpec((None, None, 6, D_MODEL), lambda b, i: (layer, row0 + b, 0, 0)),
        whole(W_A, D_MODEL), whole(W_B, D_MODEL), whole(D_MODEL, D_MODEL),
    ]
    args += [mod, w_a_bf, w_b_bf, w_out_bf]
    return pl.pallas_call(
        _mix_out_kernel,
        grid=(bsz, n // tm),
        in_specs=in_specs,
        out_specs=tok(D_MODEL),
        out_shape=jax.ShapeDtypeStruct((bsz, n, D_MODEL), F32),
        compiler_params=_params(("parallel", "parallel")),
        name="mix_out",
    )(*args)


def _ffn_kernel(*refs, has_final):
    if has_final:
        x_ref, mod_ref, n_ref, w1a_ref, w1g_ref, w2_ref, nf_ref, out_ref, h_ref = refs
    else:
        x_ref, mod_ref, n_ref, w1a_ref, w1g_ref, w2_ref, out_ref, h_ref = refs
    j = pl.program_id(2)
    rows = x_ref.shape[0]
    chunk = min(PROLOGUE_ROWS, rows)

    def partial_out(h):
        a = jnp.dot(h, w1a_ref[...], preferred_element_type=F32)
        gt = jnp.dot(h, w1g_ref[...], preferred_element_type=F32)
        return jnp.dot((_silu(a) * gt).astype(BF16), w2_ref[...], preferred_element_type=F32)

    @pl.when(j == 0)
    def _():
        for r0 in range(0, rows, chunk):
            rs = pl.ds(r0, chunk)
            h = _rms(x_ref[rs, :]) * n_ref[...]
            h = (h * (1.0 + mod_ref[4:5, :]) + mod_ref[3:4, :]).astype(BF16)
            h_ref[rs, :] = h
            out_ref[rs, :] = partial_out(h)

    last = pl.num_programs(2) - 1

    @pl.when((j > 0) & (j < last))
    def _():
        out_ref[...] += partial_out(h_ref[...])

    @pl.when(j == last)
    def _():
        for r0 in range(0, rows, chunk):
            rs = pl.ds(r0, chunk)
            acc = out_ref[rs, :] + partial_out(h_ref[rs, :])
            y = x_ref[rs, :] + mod_ref[5:6, :] * acc
            if has_final:
                y = _rms(y) * nf_ref[...]
            out_ref[rs, :] = y


def ffn(x, mod, norm_w, w_ff_in_bf, w_ff_out_bf, norm_final, layer, row0):
    bsz, n, _ = x.shape
    tm = min(TM_FFN, n)
    nj = D_FF // TN_FFN
    has_final = norm_final is not None
    in_specs = [
        pl.BlockSpec((None, tm, D_MODEL), lambda b, i, j: (b, i, 0)),
        pl.BlockSpec((None, None, 6, D_MODEL), lambda b, i, j: (layer, row0 + b, 0, 0)),
        pl.BlockSpec((None, 1, D_MODEL), lambda b, i, j: (layer, 0, 0)),
        pl.BlockSpec((None, D_MODEL, TN_FFN), lambda b, i, j: (layer, 0, j)),
        pl.BlockSpec((None, D_MODEL, TN_FFN), lambda b, i, j: (layer, 0, nj + j)),
        pl.BlockSpec((None, TN_FFN, D_MODEL), lambda b, i, j: (layer, j, 0)),
    ]
    args = [x, mod, norm_w, w_ff_in_bf, w_ff_in_bf, w_ff_out_bf]
    if has_final:
        in_specs.append(pl.BlockSpec((1, D_MODEL), lambda b, i, j: (0, 0)))
        args.append(norm_final)
    return pl.pallas_call(
        functools.partial(_ffn_kernel, has_final=has_final),
        grid=(bsz, n // tm, nj),
        in_specs=in_specs,
        out_specs=pl.BlockSpec((None, tm, D_MODEL), lambda b, i, j: (b, i, 0)),
        out_shape=jax.ShapeDtypeStruct((bsz, n, D_MODEL), F32),
        scratch_shapes=[pltpu.VMEM((tm, D_MODEL), BF16)],
        compiler_params=_params(("parallel", "parallel", "arbitrary")),
        name="ffn",
    )(*args)


def grid_pos_embed(n):
    rows = n // GRID_W
    quarter = D_MODEL // 4
    omega = 1.0 / (POS_BASE ** (jnp.arange(quarter, dtype=F32) / quarter))

    def enc(count):
        a = jnp.arange(count, dtype=F32)[:, None] * omega[None, :]
        return jnp.concatenate([jnp.sin(a), jnp.cos(a)], axis=-1)

    er = jnp.broadcast_to(enc(rows)[:, None, :], (rows, GRID_W, 2 * quarter))
    ec = jnp.broadcast_to(enc(GRID_W)[None, :, :], (rows, GRID_W, 2 * quarter))
    return jnp.concatenate([er, ec], axis=-1).reshape(n, D_MODEL)


def _add_pos_kernel(x_ref, pe_ref, out_ref):
    out_ref[...] = x_ref[...] + pe_ref[...]


def add_pos(x, pe):
    bsz, n, _ = x.shape
    tm = min(TM_IN, n)
    return pl.pallas_call(
        _add_pos_kernel,
        grid=(n // tm, bsz),
        in_specs=[pl.BlockSpec((None, tm, D_MODEL), lambda i, b: (b, i, 0)),
                  pl.BlockSpec((tm, D_MODEL), lambda i, b: (i, 0))],
        out_specs=pl.BlockSpec((None, tm, D_MODEL), lambda i, b: (b, i, 0)),
        out_shape=jax.ShapeDtypeStruct(x.shape, x.dtype),
        compiler_params=_params(("parallel", "parallel")),
        name="add_pos",
    )(x, pe)


def kernel(x_prompt, x_sample, state_hgrn, c, c_ctx, w_mod, b_mod, norm_mix, norm_ffn, w_in, lb_raw,
           g_norm, w_a, w_b, w_out, w_ff_in, w_ff_out, norm_final):
    bc, nc, _ = x_prompt.shape
    bl, nl, _ = x_sample.shape
    assert 1 + bl <= MOD_ROWS

    cvec = jnp.zeros((MOD_ROWS, D_MODEL), F32).at[0].set(c_ctx).at[1:1 + bl].set(c)
    mod = adaln_table(cvec, w_mod, b_mod)

    w_in_bf, w_a_bf, w_b_bf, w_out_bf = (w.astype(BF16) for w in (w_in, w_a, w_b, w_out))
    w_ff_in_bf, w_ff_out_bf = w_ff_in.astype(BF16), w_ff_out.astype(BF16)
    norm_mix3 = norm_mix.reshape(DEPTH, 1, D_MODEL)
    norm_ffn3 = norm_ffn.reshape(DEPTH, 1, D_MODEL)
    g_norm3 = g_norm.reshape(DEPTH, 1, DV)
    norm_final2 = norm_final.reshape(1, D_MODEL)
    lb_dir = jnp.transpose(lb_raw, (1, 0, 2))

    fidx = jnp.arange(FG, dtype=jnp.int32)
    fang = ((fidx[:, None] * fidx[None, :]) % FG).astype(F32) * (2.0 * math.pi / FG)
    cs_f = jnp.concatenate([jnp.cos(fang), jnp.sin(fang)], axis=1).astype(BF16)
    tables = {n: dft2_constants(n) if n == DFT_RADIX ** 2 else dft_tables(n) for n in {nc, nl}}

    def layer(x, seq_shape, state, sbuf, l, row0):
        bt, nt, _ = x.shape
        bs, ns = seq_shape
        p_rec, p_gate = proj_in(x, mod, norm_mix3, w_in_bf, l, row0)
        rec_s = p_rec.reshape(bs, ns, N_REC)
        gate_s = p_gate.reshape(bs, ns, N_GATE)
        o_f, s_f = scan(rec_s, lb_dir, state, l, 0, state_buf=sbuf)
        if sbuf is not None:
            sbuf = s_f
        o, s_b = scan(rec_s, lb_dir, state, l, 1, gate=(o_f, gate_s, g_norm3), state_buf=sbuf)
        if sbuf is not None:
            sbuf = s_b
        if ns == DFT_RADIX ** 2:
            z = fourier_two_stage(gate_s, cs_f, tables[ns])
        else:
            z = fourier_direct(gate_s, cs_f, *tables[ns])
        x = mix_out(o.reshape(bt, nt, W_A), p_gate, z.reshape(bt, nt, W_B),
                    x, mod, w_a_bf, w_b_bf, w_out_bf, l, row0)
        x = ffn(x, mod, norm_ffn3, w_ff_in_bf, w_ff_out_bf,
                norm_final2 if l == DEPTH - 1 else None, l, row0)
        return x, sbuf

    xc = x_prompt.reshape(1, bc * nc, D_MODEL)
    xs = add_pos(x_sample, grid_pos_embed(nl))
    state_new = jnp.zeros((bc, DEPTH, 2, H_A, DK, DV), F32)
    for l in range(DEPTH):
        xc, state_new = layer(xc, (bc, nc), None, state_new, l, 0)
        xs, _ = layer(xs, (bl, nl), state_hgrn, None, l, 1)
    return xc.reshape(bc, nc, D_MODEL), xs, state_new.astype(x_prompt.dtype)
```

```python
import functools
import math

import jax
import jax.numpy as jnp
from jax import lax
from jax.experimental import pallas as pl
from jax.experimental.pallas import tpu as pltpu

D_MODEL = 2048
DEPTH = 2
H_A = 8
DK = 128
DV = 128
W_A = H_A * DK
N_FG = 4
FG = 256
W_B = N_FG * FG
D_FF = 5632
N_IN = 5 * W_A + W_B + 2 * D_MODEL
EPS = 1e-6
GRID_W = 64
POS_BASE = 10000.0

N_REC = 4 * W_A
N_GATE = N_IN - N_REC
COL_Q = 0
COL_FF = W_A
COL_FB = 2 * W_A
COL_IV = 3 * W_A
GCOL_OG = 0
GCOL_U = W_A
GCOL_GA = W_A + W_B
GCOL_GB = W_A + W_B + D_MODEL

MOD_ROWS = 8
LANES = 128
SUBLANES = 8
VMEM_LIMIT = 56 * 1024 * 1024

TM_IN, TN_IN = 1024, 1024
PROLOGUE_ROWS = 256
TM_MIX = 256
TM_FFN, TN_FFN = 1024, 512
FOURIER_BATCH = 4
SCAN_ROWS = 512
SCAN_CHUNK = 128
SCAN_LEVELS = (64, 32, 16, 8)
SCAN_HEADS = 8
LOG2E = 1.4426950408889634

F32 = jnp.float32
BF16 = jnp.bfloat16


def _params(sem):
    return pltpu.CompilerParams(dimension_semantics=sem, vmem_limit_bytes=VMEM_LIMIT)


def _silu(x):
    return x * jax.nn.sigmoid(x)


def _rms(x):
    return x * lax.rsqrt(jnp.mean(x * x, axis=-1, keepdims=True) + EPS)


def _adaln_kernel(c_ref, w_ref, b_ref, out_ref):
    s = _silu(c_ref[...]).astype(BF16)
    out_ref[...] = jnp.dot(s, w_ref[...].astype(BF16), preferred_element_type=F32) + b_ref[...]


def adaln_table(cvec, w_mod, b_mod):
    tn = 1024
    out = pl.pallas_call(
        _adaln_kernel,
        grid=(DEPTH, 6 * D_MODEL // tn),
        in_specs=[
            pl.BlockSpec((MOD_ROWS, D_MODEL), lambda l, j: (0, 0)),
            pl.BlockSpec((None, D_MODEL, tn), lambda l, j: (l, 0, j)),
            pl.BlockSpec((None, 1, tn), lambda l, j: (l, 0, j)),
        ],
        out_specs=pl.BlockSpec((None, MOD_ROWS, tn), lambda l, j: (l, 0, j)),
        out_shape=jax.ShapeDtypeStruct((DEPTH, MOD_ROWS, 6 * D_MODEL), F32),
        compiler_params=_params(("parallel", "parallel")),
        name="adaln",
    )(cvec, w_mod, b_mod.reshape(DEPTH, 1, 6 * D_MODEL))
    return out.reshape(DEPTH, MOD_ROWS, 6, D_MODEL)


def _proj_in_kernel(x_ref, mod_ref, n_ref, w_ref, rec_ref, gate_ref, h_ref):
    j = pl.program_id(2)
    n_rec = N_REC // w_ref.shape[1]
    rows = x_ref.shape[0]
    chunk = min(PROLOGUE_ROWS, rows)

    @pl.when(j == 0)
    def _():
        for r0 in range(0, rows, chunk):
            rs = pl.ds(r0, chunk)
            h = _rms(x_ref[rs, :]) * n_ref[...]
            h = (h * (1.0 + mod_ref[1:2, :]) + mod_ref[0:1, :]).astype(BF16)
            h_ref[rs, :] = h
            rec_ref[rs, :] = jnp.dot(h, w_ref[...], preferred_element_type=F32)

    @pl.when((j > 0) & (j < n_rec))
    def _():
        rec_ref[...] = jnp.dot(h_ref[...], w_ref[...], preferred_element_type=F32)

    @pl.when(j >= n_rec)
    def _():
        gate_ref[...] = jnp.dot(h_ref[...], w_ref[...], preferred_element_type=F32).astype(BF16)


def proj_in(x, mod, norm_w, w_in_bf, layer, row0):
    bsz, n, _ = x.shape
    tm = min(TM_IN, n)
    tn = TN_IN
    n_rec = N_REC // tn
    return pl.pallas_call(
        _proj_in_kernel,
        grid=(bsz, n // tm, N_IN // tn),
        in_specs=[
            pl.BlockSpec((None, tm, D_MODEL), lambda b, i, j: (b, i, 0)),
            pl.BlockSpec((None, None, 6, D_MODEL), lambda b, i, j: (layer, row0 + b, 0, 0)),
            pl.BlockSpec((None, 1, D_MODEL), lambda b, i, j: (layer, 0, 0)),
            pl.BlockSpec((None, D_MODEL, tn), lambda b, i, j: (layer, 0, j)),
        ],
        out_specs=[
            pl.BlockSpec((None, tm, tn), lambda b, i, j: (b, i, jnp.minimum(j, n_rec - 1))),
            pl.BlockSpec((None, tm, tn), lambda b, i, j: (b, i, jnp.maximum(j - n_rec, 0))),
        ],
        out_shape=[jax.ShapeDtypeStruct((bsz, n, N_REC), F32),
                   jax.ShapeDtypeStruct((bsz, n, N_GATE), BF16)],
        scratch_shapes=[pltpu.VMEM((tm, D_MODEL), BF16)],
        compiler_params=_params(("parallel", "parallel", "arbitrary")),
        name="proj_in",
    )(x, mod, norm_w, w_in_bf)


def _nt(a, b):
    return lax.dot_general(a, b, (((1,), (1,)), ((), ())), preferred_element_type=F32)


def _tn(a, b):
    return lax.dot_general(a, b, (((0,), (0,)), ((), ())), preferred_element_type=F32)


def _chunk(q_ref, k_ref, b_ref, v_ref, m_ref, base, sl, st, reverse, pair_masks):
    def rows(ref, r0, n):
        return ref[pl.ds(base + r0, n), sl]

    def row(ref, r):
        return ref[pl.ds(base + r, 1), sl]

    q = rows(q_ref, 0, SCAN_CHUNK)
    k = rows(k_ref, 0, SCAN_CHUNK)
    b = rows(b_ref, 0, SCAN_CHUNK)
    vb = rows(v_ref, 0, SCAN_CHUNK).astype(BF16)
    b_end = row(b_ref, 0 if reverse else SCAN_CHUNK - 1)

    o = _nt((q * jnp.exp2(b)).astype(BF16), st.astype(BF16))

    levels = []
    for h in SCAN_LEVELS:
        z = jnp.zeros((h, LANES), F32)
        qp, kp = [], []
        for r0 in range(0, SCAN_CHUNK, 2 * h):
            if reverse:
                beta = row(b_ref, r0 + h)
                qp += [rows(q_ref, r0, h) * jnp.exp2(rows(b_ref, r0, h) - beta), z]
                kp += [z, rows(k_ref, r0 + h, h) * jnp.exp2(beta - rows(b_ref, r0 + h, h))]
            else:
                beta = row(b_ref, r0 + h - 1)
                kp += [rows(k_ref, r0, h) * jnp.exp2(beta - rows(b_ref, r0, h)), z]
                qp += [z, rows(q_ref, r0 + h, h) * jnp.exp2(rows(b_ref, r0 + h, h) - beta)]
        levels.append(_nt(jnp.concatenate(qp, axis=0).astype(BF16),
                          jnp.concatenate(kp, axis=0).astype(BF16)))

    parts = []
    for r0 in range(0, SCAN_CHUNK, SUBLANES):
        qv = rows(q_ref, r0, SUBLANES)
        bv = rows(b_ref, r0, SUBLANES)
        near = jnp.zeros((SUBLANES, LANES), F32)
        for j in range(SUBLANES):
            shape = (SUBLANES, LANES)
            kj = jnp.broadcast_to(row(k_ref, r0 + j), shape)
            bj = jnp.broadcast_to(row(b_ref, r0 + j), shape)
            w = jnp.sum(qv * kj * jnp.exp2(bv - bj), axis=1, keepdims=True)
            near = jnp.where(pair_masks[j], w, near)
        rs = slice(r0, r0 + SUBLANES)
        acc = near * m_ref[len(SCAN_LEVELS), rs, :]
        for li, h in enumerate(SCAN_LEVELS):
            if ((r0 & h) == 0) == reverse:
                acc = acc + levels[li][rs] * m_ref[li, rs, :]
        parts.append(acc)
    a = jnp.concatenate(parts, axis=0)
    o = o + jnp.dot(a.astype(BF16), vb, preferred_element_type=F32)

    khat = (k * jnp.exp2(b_end - b)).astype(BF16)
    st = st * jnp.exp2(b_end) + _tn(vb, khat)
    return o, st


def _scan_kernel(*refs, layer, reverse, has_s0, has_gate, has_sbuf):
    refs = list(refs)
    q_ref, x_ref, v_ref, lb_ref, m_ref = refs[:5]
    del refs[:5]
    s0_ref = refs.pop(0) if has_s0 else None
    of_ref, og_ref, gn_ref = (refs.pop(0), refs.pop(0), refs.pop(0)) if has_gate else (None, None, None)
    if has_sbuf:
        refs.pop(0)
    o_ref, sfin_ref, st_ref, k_scr, b_scr = refs
    i = pl.program_id(2)
    nch = q_ref.shape[0] // SCAN_CHUNK
    width = q_ref.shape[1]

    @pl.when(i == 0)
    def _():
        for hh in range(SCAN_HEADS):
            if has_s0:
                st_ref[hh] = s0_ref[hh].T
            else:
                st_ref[hh] = jnp.zeros((DV, DK), F32)

    t_idx = lax.broadcasted_iota(jnp.int32, (SCAN_CHUNK, SCAN_CHUNK), 0)
    s_idx = lax.broadcasted_iota(jnp.int32, (SCAN_CHUNK, SCAN_CHUNK), 1)
    incl = (s_idx >= t_idx) if reverse else (s_idx <= t_idx)
    tmat = jnp.where(incl, 1.0, 0.0).astype(BF16)
    sub = lax.broadcasted_iota(jnp.int32, (SUBLANES, LANES), 0)
    lane = lax.broadcasted_iota(jnp.int32, (SUBLANES, LANES), 1)
    pair_masks = [((lane & (SUBLANES - 1)) == j) & ((sub <= j) if reverse else (sub >= j))
                  for j in range(SUBLANES)]

    lbr = lb_ref[...]
    e = jnp.exp(lbr - jnp.max(lbr, axis=0, keepdims=True))
    lb_all = jnp.sum(e[1:layer + 1], axis=0, keepdims=True) / jnp.sum(e, axis=0, keepdims=True)

    for c in (range(nch - 1, -1, -1) if reverse else range(nch)):
        base = SCAN_CHUNK * c
        rs = pl.ds(base, SCAN_CHUNK)
        f = jax.nn.sigmoid(x_ref[rs, :])
        if layer > 0:
            f = lb_all + (1.0 - lb_all) * f
        g = jnp.log(f) * LOG2E
        ghi = g.astype(BF16)
        glo = (g - ghi.astype(F32)).astype(BF16)
        b2 = jnp.dot(tmat, jnp.concatenate([ghi, glo], axis=1), preferred_element_type=F32)
        k_scr[rs, :] = 1.0 - f
        b_scr[rs, :] = b2[:, :width] + b2[:, width:]
        for hh in range(SCAN_HEADS):
            sl = slice(LANES * hh, LANES * (hh + 1))
            o, st = _chunk(q_ref, k_scr, b_scr, v_ref, m_ref, base, sl, st_ref[hh], reverse,
                           pair_masks)
            if has_gate:
                o = _rms(o + of_ref[rs, sl]) * gn_ref[...]
                o_ref[rs, sl] = (o * _silu(og_ref[rs, sl].astype(F32))).astype(BF16)
            else:
                o_ref[rs, sl] = o
            st_ref[hh] = st

    @pl.when(i == pl.num_programs(2) - 1)
    def _():
        for hh in range(SCAN_HEADS):
            sfin_ref[hh] = st_ref[hh].T


def scan_masks(reverse):
    t = jnp.arange(SCAN_CHUNK, dtype=jnp.int32)[:, None]
    s = jnp.arange(SCAN_CHUNK, dtype=jnp.int32)[None, :]
    diff = t ^ s
    strict = (s > t) if reverse else (s < t)
    masks = [(diff >= h) & (diff < 2 * h) & strict for h in SCAN_LEVELS] + [diff < SUBLANES]
    return jnp.stack(masks).astype(F32)


def scan(p, lb_dir, state, layer, direction, gate=None, state_buf=None):
    bsz, n, _ = p.shape
    reverse = direction == 1
    srows = min(SCAN_ROWS, n)
    nblk = n // srows
    wblk = LANES * SCAN_HEADS
    has_s0 = state is not None
    has_gate = gate is not None
    has_sbuf = state_buf is not None

    def rowblk(i):
        return nblk - 1 - i if reverse else i

    def col(c0):
        return lambda b, h, i: (b, rowblk(i), c0 // wblk + h)

    in_specs = [
        pl.BlockSpec((None, srows, wblk), col(COL_Q)),
        pl.BlockSpec((None, srows, wblk), col(COL_FB if reverse else COL_FF)),
        pl.BlockSpec((None, srows, wblk), col(COL_IV)),
        pl.BlockSpec((None, DEPTH, wblk), lambda b, h, i: (direction, 0, h)),
        pl.BlockSpec((len(SCAN_LEVELS) + 1, SCAN_CHUNK, SCAN_CHUNK), lambda b, h, i: (0, 0, 0)),
    ]
    args = [p, p, p, lb_dir, scan_masks(reverse)]
    if has_s0:
        in_specs.append(pl.BlockSpec((None, None, None, SCAN_HEADS, DK, DV),
                                     lambda b, h, i: (b, layer, direction, h, 0, 0)))
        args.append(state)
    if has_gate:
        o_other, p_gate, g_norm = gate
        in_specs += [
            pl.BlockSpec((None, srows, wblk), lambda b, h, i: (b, rowblk(i), h)),
            pl.BlockSpec((None, srows, wblk), lambda b, h, i: (b, rowblk(i), GCOL_OG // wblk + h)),
            pl.BlockSpec((None, 1, DV), lambda b, h, i: (layer, 0, 0)),
        ]
        args += [o_other, p_gate, g_norm]
    aliases = {}
    if has_sbuf:
        aliases = {len(args): 1}
        in_specs.append(pl.BlockSpec(memory_space=pl.ANY))
        args.append(state_buf)
        s_spec = pl.BlockSpec((None, None, None, SCAN_HEADS, DK, DV),
                              lambda b, h, i: (b, layer, direction, h, 0, 0))
        s_shape = jax.ShapeDtypeStruct(state_buf.shape, F32)
    else:
        s_spec = pl.BlockSpec((None, SCAN_HEADS, DK, DV), lambda b, h, i: (b, h, 0, 0))
        s_shape = jax.ShapeDtypeStruct((bsz, H_A, DK, DV), F32)
    return pl.pallas_call(
        functools.partial(_scan_kernel, layer=layer, reverse=reverse, has_s0=has_s0,
                          has_gate=has_gate, has_sbuf=has_sbuf),
        grid=(bsz, H_A // SCAN_HEADS, nblk),
        in_specs=in_specs,
        out_specs=[
            pl.BlockSpec((None, srows, wblk), lambda b, h, i: (b, rowblk(i), h)),
            s_spec,
        ],
        out_shape=[
            jax.ShapeDtypeStruct((bsz, n, W_A), BF16 if has_gate else F32),
            s_shape,
        ],
        input_output_aliases=aliases,
        scratch_shapes=[pltpu.VMEM((SCAN_HEADS, DV, DK), F32),
                        pltpu.VMEM((srows, wblk), F32),
                        pltpu.VMEM((srows, wblk), F32)],
        compiler_params=_params(("parallel", "parallel", "arbitrary")),
        name="scan_bwd" if reverse else "scan_fwd",
    )(*args)


def _dft_fused_kernel(u_ref, cs_ref, l1_ref, tc_ref, ts_ref, l2_ref, z_ref, yr_scr, yi_scr, *, scale):
    j = pl.program_id(1)
    nstep = DFT_RADIX // DFT_STEP
    for s in range(nstep):
        @pl.when(j == s)
        def _(s=s):
            _dft_stage1_kernel(u_ref, cs_ref, l1_ref, tc_ref, ts_ref,
                               yr_scr.at[:, pl.ds(DFT_STEP * s, DFT_STEP), :],
                               yi_scr.at[:, pl.ds(DFT_STEP * s, DFT_STEP), :])

        @pl.when(j == nstep + s)
        def _(s=s):
            _dft_stage2_kernel(yr_scr.at[pl.ds(DFT_STEP * s, DFT_STEP)], yi_scr.at[pl.ds(DFT_STEP * s, DFT_STEP)],
                               l2_ref, z_ref, scale=scale)


def fourier_fused(p, cs_f, consts):
    bsz, n, _ = p.shape
    m = DFT_RADIX
    l1, l2, tw_c, tw_s = consts
    p4 = p.reshape(bsz, m, m, N_GATE)
    nstep = m // DFT_STEP
    per_step = DFT_STEP // DFT_GROUP
    first = lambda j: jnp.minimum(j, nstep - 1)
    second = lambda j: jnp.maximum(j - nstep, 0)
    twspec = pl.BlockSpec((per_step, m * DFT_GROUP, LANES), lambda b, j: (first(j), 0, 0))
    z = pl.pallas_call(
        functools.partial(_dft_fused_kernel, scale=1.0 / math.sqrt(n * FG)),
        grid=(bsz, 2 * nstep),
        in_specs=[
            pl.BlockSpec((None, m, DFT_STEP, W_B), lambda b, j: (b, 0, first(j), GCOL_U // W_B)),
            pl.BlockSpec((FG, 2 * FG), lambda b, j: (0, 0)),
            pl.BlockSpec(l1.shape, lambda b, j: (0, 0)),
            twspec, twspec,
            pl.BlockSpec(l2.shape, lambda b, j: (0, 0)),
        ],
        out_specs=pl.BlockSpec((None, m, DFT_STEP, W_B), lambda b, j: (b, 0, second(j), 0)),
        out_shape=jax.ShapeDtypeStruct((bsz, m, m, W_B), BF16),
        scratch_shapes=[pltpu.VMEM((m, m, W_B), BF16), pltpu.VMEM((m, m, W_B), BF16)],
        compiler_params=_params(("parallel", "arbitrary")),
        name="dft_fused",
    )(p4, cs_f, l1, tw_c, tw_s, l2)
    return z.reshape(bsz, n, W_B)


def _fourier_direct_kernel(u_ref, cs_ref, c_ref, s_ref, z_ref, *, scale):
    cs = cs_ref[...]
    for i in range(u_ref.shape[0]):
        a_parts, b_parts = [], []
        for g in range(N_FG):
            r = jnp.dot(u_ref[i, :, FG * g:FG * (g + 1)], cs, preferred_element_type=F32)
            a_parts.append(r[:, :FG].astype(BF16))
            b_parts.append(r[:, FG:].astype(BF16))
        acc = jnp.dot(c_ref[...], jnp.concatenate(a_parts, axis=1), preferred_element_type=F32)
        acc = acc - jnp.dot(s_ref[...], jnp.concatenate(b_parts, axis=1), preferred_element_type=F32)
        z_ref[i] = (acc * scale).astype(BF16)


def fourier_direct(p, cs_f, cos_n, sin_n):
    bsz, n, _ = p.shape
    bb = math.gcd(FOURIER_BATCH, bsz)
    return pl.pallas_call(
        functools.partial(_fourier_direct_kernel, scale=1.0 / math.sqrt(n * FG)),
        grid=(bsz // bb,),
        in_specs=[
            pl.BlockSpec((bb, n, W_B), lambda b: (b, 0, GCOL_U // W_B)),
            pl.BlockSpec((FG, 2 * FG), lambda b: (0, 0)),
            pl.BlockSpec((n, n), lambda b: (0, 0)),
            pl.BlockSpec((n, n), lambda b: (0, 0)),
        ],
        out_specs=pl.BlockSpec((bb, n, W_B), lambda b: (b, 0, 0)),
        out_shape=jax.ShapeDtypeStruct((bsz, n, W_B), BF16),
        compiler_params=_params(("parallel",)),
        name="fourier_direct",
    )(p, cs_f, cos_n, sin_n)


def dft_tables(n):
    idx = jnp.arange(n, dtype=jnp.int32)
    ang = ((idx[:, None] * idx[None, :]) % n).astype(F32) * (2.0 * math.pi / n)
    return jnp.cos(ang).astype(BF16), jnp.sin(ang).astype(BF16)


DFT_RADIX = 64
DFT_GROUP = 8
DFT_STEP = 16


def _kron_rows(mat, inner):
    r, c = mat.shape
    eye = jnp.eye(inner, dtype=mat.dtype)
    return (mat[:, None, :, None] * eye[None, :, None, :]).reshape(r * inner, c * inner)


def dft2_constants(n):
    m = DFT_RADIX
    idx = jnp.arange(m, dtype=jnp.int32)
    ang = ((idx[:, None] * idx[None, :]) % m).astype(F32) * (2.0 * math.pi / m)
    cm, sm = jnp.cos(ang), jnp.sin(ang)
    ck, sk = _kron_rows(cm, DFT_GROUP), _kron_rows(sm, DFT_GROUP)
    l1 = jnp.concatenate([jnp.concatenate([ck, -sk], axis=1),
                          jnp.concatenate([sk, ck], axis=1)], axis=0).astype(BF16)
    eye = jnp.eye(DFT_GROUP, dtype=F32)
    c2 = (cm[:, None, None, :] * eye[None, :, :, None]).reshape(m * DFT_GROUP, DFT_GROUP * m)
    s2 = (sm[:, None, None, :] * eye[None, :, :, None]).reshape(m * DFT_GROUP, DFT_GROUP * m)
    l2 = jnp.concatenate([c2, -s2], axis=1).astype(BF16)
    k1 = jnp.arange(m, dtype=jnp.int32)[None, :, None]
    n2 = (jnp.arange(m // DFT_GROUP, dtype=jnp.int32)[:, None, None] * DFT_GROUP
          + jnp.arange(DFT_GROUP, dtype=jnp.int32)[None, None, :])
    tang = ((k1 * n2) % n).astype(F32) * (2.0 * math.pi / n)
    tang = jnp.broadcast_to(tang.reshape(m // DFT_GROUP, m * DFT_GROUP, 1),
                            (m // DFT_GROUP, m * DFT_GROUP, LANES))
    return l1, l2, jnp.cos(tang), jnp.sin(tang)


def _dft_stage1_kernel(u_ref, cs_ref, l1_ref, tc_ref, ts_ref, yr_ref, yi_ref):
    m = DFT_RADIX
    rows = m * DFT_GROUP
    cs = cs_ref[...]
    l1 = l1_ref[...]
    yr_parts, yi_parts = [], []
    u_all = u_ref[...].astype(F32)
    for hf in range(DFT_STEP // DFT_GROUP):
        u = u_all[:, DFT_GROUP * hf:DFT_GROUP * (hf + 1), :].reshape(rows, W_B)
        a_parts, b_parts = [], []
        for g in range(N_FG):
            r = jnp.dot(u[:, FG * g:FG * (g + 1)].astype(BF16), cs, preferred_element_type=F32)
            a_parts.append(r[:, :FG].astype(BF16))
            b_parts.append(r[:, FG:].astype(BF16))
        x = jnp.concatenate([jnp.concatenate(a_parts, axis=1), jnp.concatenate(b_parts, axis=1)], axis=0)
        y = jnp.dot(l1, x, preferred_element_type=F32)
        yr, yi = y[:rows], y[rows:]
        tc, ts = tc_ref[hf], ts_ref[hf]
        pr, pi = [], []
        for c in range(W_B // LANES):
            sl = slice(LANES * c, LANES * (c + 1))
            pr.append(yr[:, sl] * tc - yi[:, sl] * ts)
            pi.append(yr[:, sl] * ts + yi[:, sl] * tc)
        yr_parts.append(jnp.concatenate(pr, axis=1).reshape(m, DFT_GROUP, W_B))
        yi_parts.append(jnp.concatenate(pi, axis=1).reshape(m, DFT_GROUP, W_B))
    yr_ref[...] = jnp.concatenate(yr_parts, axis=1).astype(BF16)
    yi_ref[...] = jnp.concatenate(yi_parts, axis=1).astype(BF16)


def _dft_stage2_kernel(yr_ref, yi_ref, l2_ref, z_ref, *, scale):
    m = DFT_RADIX
    rows = m * DFT_GROUP
    l2 = l2_ref[...]
    parts = []
    for hf in range(DFT_STEP // DFT_GROUP):
        sl = slice(DFT_GROUP * hf, DFT_GROUP * (hf + 1))
        x = jnp.concatenate([yr_ref[sl].reshape(rows, W_B), yi_ref[sl].reshape(rows, W_B)], axis=0)
        z = jnp.dot(l2, x, preferred_element_type=F32) * scale
        parts.append(z.reshape(m, DFT_GROUP, W_B))
    z_ref[...] = jnp.concatenate(parts, axis=1).astype(BF16)


def fourier_two_stage(p, cs_f, consts):
    bsz, n, _ = p.shape
    m = DFT_RADIX
    l1, l2, tw_c, tw_s = consts
    p4 = p.reshape(bsz, m, m, N_GATE)
    nstep = m // DFT_STEP
    per_step = DFT_STEP // DFT_GROUP
    ybuf = jax.ShapeDtypeStruct((bsz, m, m, W_B), BF16)
    yspec = pl.BlockSpec((None, m, DFT_STEP, W_B), lambda b, j: (b, 0, j, 0))
    twspec = pl.BlockSpec((per_step, m * DFT_GROUP, LANES), lambda b, j: (j, 0, 0))
    yr, yi = pl.pallas_call(
        _dft_stage1_kernel,
        grid=(bsz, nstep),
        in_specs=[
            pl.BlockSpec((None, m, DFT_STEP, W_B), lambda b, j: (b, 0, j, GCOL_U // W_B)),
            pl.BlockSpec((FG, 2 * FG), lambda b, j: (0, 0)),
            pl.BlockSpec(l1.shape, lambda b, j: (0, 0)),
            twspec, twspec,
        ],
        out_specs=[yspec, yspec],
        out_shape=[ybuf, ybuf],
        compiler_params=_params(("parallel", "parallel")),
        name="dft_stage1",
    )(p4, cs_f, l1, tw_c, tw_s)
    xspec = pl.BlockSpec((None, DFT_STEP, m, W_B), lambda b, i: (b, i, 0, 0))
    z = pl.pallas_call(
        functools.partial(_dft_stage2_kernel, scale=1.0 / math.sqrt(n * FG)),
        grid=(bsz, nstep),
        in_specs=[xspec, xspec, pl.BlockSpec(l2.shape, lambda b, i: (0, 0))],
        out_specs=pl.BlockSpec((None, m, DFT_STEP, W_B), lambda b, i: (b, 0, i, 0)),
        out_shape=jax.ShapeDtypeStruct((bsz, m, m, W_B), BF16),
        compiler_params=_params(("parallel", "parallel")),
        name="dft_stage2",
    )(yr, yi, l2)
    return z.reshape(bsz, n, W_B)


def _mix_out_kernel(o_ref, z_ref, ga_ref, gb_ref, x_ref, mod_ref, wa_ref, wb_ref, wo_ref, out_ref):
    ya = jnp.dot(o_ref[...], wa_ref[...], preferred_element_type=F32)
    yb = jnp.dot(z_ref[...], wb_ref[...], preferred_element_type=F32)
    merged = (jax.nn.sigmoid(ga_ref[...].astype(F32)) * ya
              + jax.nn.sigmoid(gb_ref[...].astype(F32)) * yb)
    m = jnp.dot(merged.astype(BF16), wo_ref[...], preferred_element_type=F32)
    out_ref[...] = x_ref[...] + mod_ref[2:3, :] * m


def mix_out(o, p_gate, z, x, mod, w_a_bf, w_b_bf, w_out_bf, layer, row0):
    bsz, n, _ = x.shape
    tm = min(TM_MIX, n)

    def tok(width, cblk=0):
        return pl.BlockSpec((None, tm, width), lambda b, i: (b, i, cblk))

    def whole(rows, cols):
        return pl.BlockSpec((None, rows, cols), lambda b, i: (layer, 0, 0),
                            pipeline_mode=pl.Buffered(1))

    in_specs = [tok(W_A), tok(W_B), tok(D_MODEL, GCOL_GA // D_MODEL), tok(D_MODEL, GCOL_GB // D_MODEL),
                tok(D_MODEL)]
    args = [o, z, p_gate, p_gate, x]
    in_specs += [
        pl.BlockSpec((None, None, 6, D_MODEL), lambda b, i: (layer, row0 + b, 0, 0)),
        whole(W_A, D_MODEL), whole(W_B, D_MODEL), whole(D_MODEL, D_MODEL),
    ]
    args += [mod, w_a_bf, w_b_bf, w_out_bf]
    return pl.pallas_call(
        _mix_out_kernel,
        grid=(bsz, n // tm),
        in_specs=in_specs,
        out_specs=tok(D_MODEL),
        out_shape=jax.ShapeDtypeStruct((bsz, n, D_MODEL), F32),
        compiler_params=_params(("parallel", "parallel")),
        name="mix_out",
    )(*args)


def _ffn_kernel(*refs, has_final):
    if has_final:
        x_ref, mod_ref, n_ref, w1a_ref, w1g_ref, w2_ref, nf_ref, out_ref, h_ref = refs
    else:
        x_ref, mod_ref, n_ref, w1a_ref, w1g_ref, w2_ref, out_ref, h_ref = refs
    j = pl.program_id(2)
    rows = x_ref.shape[0]
    chunk = min(PROLOGUE_ROWS, rows)

    def partial_out(h):
        a = jnp.dot(h, w1a_ref[...], preferred_element_type=F32)
        gt = jnp.dot(h, w1g_ref[...], preferred_element_type=F32)
        return jnp.dot((_silu(a) * gt).astype(BF16), w2_ref[...], preferred_element_type=F32)

    @pl.when(j == 0)
    def _():
        for r0 in range(0, rows, chunk):
            rs = pl.ds(r0, chunk)
            h = _rms(x_ref[rs, :]) * n_ref[...]
            h = (h * (1.0 + mod_ref[4:5, :]) + mod_ref[3:4, :]).astype(BF16)
            h_ref[rs, :] = h
            out_ref[rs, :] = partial_out(h)

    last = pl.num_programs(2) - 1

    @pl.when((j > 0) & (j < last))
    def _():
        out_ref[...] += partial_out(h_ref[...])

    @pl.when(j == last)
    def _():
        for r0 in range(0, rows, chunk):
            rs = pl.ds(r0, chunk)
            acc = out_ref[rs, :] + partial_out(h_ref[rs, :])
            y = x_ref[rs, :] + mod_ref[5:6, :] * acc
            if has_final:
                y = _rms(y) * nf_ref[...]
            out_ref[rs, :] = y


def ffn(x, mod, norm_w, w_ff_in_bf, w_ff_out_bf, norm_final, layer, row0):
    bsz, n, _ = x.shape
    tm = min(TM_FFN, n)
    nj = D_FF // TN_FFN
    has_final = norm_final is not None
    in_specs = [
        pl.BlockSpec((None, tm, D_MODEL), lambda b, i, j: (b, i, 0)),
        pl.BlockSpec((None, None, 6, D_MODEL), lambda b, i, j: (layer, row0 + b, 0, 0)),
        pl.BlockSpec((None, 1, D_MODEL), lambda b, i, j: (layer, 0, 0)),
        pl.BlockSpec((None, D_MODEL, TN_FFN), lambda b, i, j: (layer, 0, j)),
        pl.BlockSpec((None, D_MODEL, TN_FFN), lambda b, i, j: (layer, 0, nj + j)),
        pl.BlockSpec((None, TN_FFN, D_MODEL), lambda b, i, j: (layer, j, 0)),
    ]
    args = [x, mod, norm_w, w_ff_in_bf, w_ff_in_bf, w_ff_out_bf]
    if has_final:
        in_specs.append(pl.BlockSpec((1, D_MODEL), lambda b, i, j: (0, 0)))
        args.append(norm_final)
    return pl.pallas_call(
        functools.partial(_ffn_kernel, has_final=has_final),
        grid=(bsz, n // tm, nj),
        in_specs=in_specs,
        out_specs=pl.BlockSpec((None, tm, D_MODEL), lambda b, i, j: (b, i, 0)),
        out_shape=jax.ShapeDtypeStruct((bsz, n, D_MODEL), F32),
        scratch_shapes=[pltpu.VMEM((tm, D_MODEL), BF16)],
        compiler_params=_params(("parallel", "parallel", "arbitrary")),
        name="ffn",
    )(*args)


def grid_pos_embed(n):
    rows = n // GRID_W
    quarter = D_MODEL // 4
    omega = 1.0 / (POS_BASE ** (jnp.arange(quarter, dtype=F32) / quarter))

    def enc(count):
        a = jnp.arange(count, dtype=F32)[:, None] * omega[None, :]
        return jnp.concatenate([jnp.sin(a), jnp.cos(a)], axis=-1)

    er = jnp.broadcast_to(enc(rows)[:, None, :], (rows, GRID_W, 2 * quarter))
    ec = jnp.broadcast_to(enc(GRID_W)[None, :, :], (rows, GRID_W, 2 * quarter))
    return jnp.concatenate([er, ec], axis=-1).reshape(n, D_MODEL)


def _add_pos_kernel(x_ref, pe_ref, out_ref):
    out_ref[...] = x_ref[...] + pe_ref[...]


def add_pos(x, pe):
    bsz, n, _ = x.shape
    tm = min(TM_IN, n)
    return pl.pallas_call(
        _add_pos_kernel,
        grid=(n // tm, bsz),
        in_specs=[pl.BlockSpec((None, tm, D_MODEL), lambda i, b: (b, i, 0)),
                  pl.BlockSpec((tm, D_MODEL), lambda i, b: (i, 0))],
        out_specs=pl.BlockSpec((None, tm, D_MODEL), lambda i, b: (b, i, 0)),
        out_shape=jax.ShapeDtypeStruct(x.shape, x.dtype),
        compiler_params=_params(("parallel", "parallel")),
        name="add_pos",
    )(x, pe)


def kernel(x_prompt, x_sample, state_hgrn, c, c_ctx, w_mod, b_mod, norm_mix, norm_ffn, w_in, lb_raw,
           g_norm, w_a, w_b, w_out, w_ff_in, w_ff_out, norm_final):
    bc, nc, _ = x_prompt.shape
    bl, nl, _ = x_sample.shape
    assert 1 + bl <= MOD_ROWS

    cvec = jnp.zeros((MOD_ROWS, D_MODEL), F32).at[0].set(c_ctx).at[1:1 + bl].set(c)
    mod = adaln_table(cvec, w_mod, b_mod)

    w_in_bf, w_a_bf, w_b_bf, w_out_bf = (w.astype(BF16) for w in (w_in, w_a, w_b, w_out))
    w_ff_in_bf, w_ff_out_bf = w_ff_in.astype(BF16), w_ff_out.astype(BF16)
    norm_mix3 = norm_mix.reshape(DEPTH, 1, D_MODEL)
    norm_ffn3 = norm_ffn.reshape(DEPTH, 1, D_MODEL)
    g_norm3 = g_norm.reshape(DEPTH, 1, DV)
    norm_final2 = norm_final.reshape(1, D_MODEL)
    lb_dir = jnp.transpose(lb_raw, (1, 0, 2))

    fidx = jnp.arange(FG, dtype=jnp.int32)
    fang = ((fidx[:, None] * fidx[None, :]) % FG).astype(F32) * (2.0 * math.pi / FG)
    cs_f = jnp.concatenate([jnp.cos(fang), jnp.sin(fang)], axis=1).astype(BF16)
    tables = {n: dft2_constants(n) if n == DFT_RADIX ** 2 else dft_tables(n) for n in {nc, nl}}

    def layer(x, seq_shape, state, sbuf, l, row0):
        bt, nt, _ = x.shape
        bs, ns = seq_shape
        p_rec, p_gate = proj_in(x, mod, norm_mix3, w_in_bf, l, row0)
        rec_s = p_rec.reshape(bs, ns, N_REC)
        gate_s = p_gate.reshape(bs, ns, N_GATE)
        o_f, s_f = scan(rec_s, lb_dir, state, l, 0, state_buf=sbuf)
        if sbuf is not None:
            sbuf = s_f
        o, s_b = scan(rec_s, lb_dir, state, l, 1, gate=(o_f, gate_s, g_norm3), state_buf=sbuf)
        if sbuf is not None:
            sbuf = s_b
        if ns == DFT_RADIX ** 2:
            z = fourier_fused(gate_s, cs_f, tables[ns])
        else:
            z = fourier_direct(gate_s, cs_f, *tables[ns])
        x = mix_out(o.reshape(bt, nt, W_A), p_gate, z.reshape(bt, nt, W_B),
                    x, mod, w_a_bf, w_b_bf, w_out_bf, l, row0)
        x = ffn(x, mod, norm_ffn3, w_ff_in_bf, w_ff_out_bf,
                norm_final2 if l == DEPTH - 1 else None, l, row0)
        return x, sbuf

    xc = x_prompt.reshape(1, bc * nc, D_MODEL)
    xs = add_pos(x_sample, grid_pos_embed(nl))
    state_new = jnp.zeros((bc, DEPTH, 2, H_A, DK, DV), F32)
    for l in range(DEPTH):
        xc, state_new = layer(xc, (bc, nc), None, state_new, l, 0)
        xs, _ = layer(xs, (bl, nl), state_hgrn, None, l, 1)
    return xc.reshape(bc, nc, D_MODEL), xs, state_new.astype(x_prompt.dtype)
```
